```python
import math
import jax, jax.numpy as jnp
from jax import lax
import numpy as np


D_MODEL = 1024
BATCH = 8
SEQ = 8192
DEPTH = 1

MIX_WIDTH = D_MODEL
DIFF_WIDTH = MIX_WIDTH // 2
DIFF_QK_DIM = 64
DIFF_V_DIM = 2 * DIFF_QK_DIM
DIFF_HEADS = DIFF_WIDTH // DIFF_V_DIM
RWKV_WIDTH = MIX_WIDTH - DIFF_WIDTH
RWKV_HEAD_DIM = 64
RWKV_HEADS = RWKV_WIDTH // RWKV_HEAD_DIM
DECAY_LORA = 64
ICLR_LORA = 64
GATE_LORA = 128
N_DIR = 2
DIFF_COLS = 3 * DIFF_WIDTH
RWKV_COLS = 3 * RWKV_WIDTH + N_DIR * DECAY_LORA + N_DIR * ICLR_LORA + GATE_LORA
IN_COLS = DIFF_COLS + RWKV_COLS
D_FF = 2816
CONV_WIDTH = 3
ROPE_THETA = 10000.0
Q_BLOCK = 128
NORM_EPS = 1e-6
GN_EPS = 64e-5

kernel_name = "hybrid_diffattn_rwkv7_encoder"


def rms_norm(x, g):
    xf = x.astype(jnp.float32)
    y = xf * lax.rsqrt(jnp.mean(xf * xf, axis=-1, keepdims=True) + NORM_EPS)
    return (y * g.astype(jnp.float32)).astype(x.dtype)


def shift_prev(z):
    return jnp.pad(z[:, :-1], ((0, 0), (1, 0), (0, 0)))


def shift_next(z):
    return jnp.pad(z[:, 1:], ((0, 0), (0, 1), (0, 0)))


def rope(t, pos):
    half = t.shape[-1] // 2
    inv = ROPE_THETA ** (-jnp.arange(half, dtype=jnp.float32) / half)
    ang = pos[:, None] * inv[None, :]
    cos = jnp.cos(ang)[None, :, None, None, :]
    sin = jnp.sin(ang)[None, :, None, None, :]
    t1 = t[..., :half].astype(jnp.float32)
    t2 = t[..., half:].astype(jnp.float32)
    return jnp.concatenate([t1 * cos - t2 * sin, t2 * cos + t1 * sin], axis=-1).astype(t.dtype)


def diff_attention(q, k, v, lam_q1, lam_k1, lam_q2, lam_k2, subln_g, lambda_init):
    B, S = q.shape[0], q.shape[1]
    pos = jnp.arange(S, dtype=jnp.float32)
    q = rope(q.reshape(B, S, DIFF_HEADS, 2, DIFF_QK_DIM), pos)
    k = rope(k.reshape(B, S, DIFF_HEADS, 2, DIFF_QK_DIM), pos)
    v = v.reshape(B, S, DIFF_HEADS, DIFF_V_DIM)
    lam = (jnp.exp(jnp.sum(lam_q1.astype(jnp.float32) * lam_k1.astype(jnp.float32)))
           - jnp.exp(jnp.sum(lam_q2.astype(jnp.float32) * lam_k2.astype(jnp.float32)))
           + lambda_init)
    scale = DIFF_QK_DIM ** -0.5
    n_blk = S // Q_BLOCK
    qb = q.reshape(B, n_blk, Q_BLOCK, DIFF_HEADS, 2, DIFF_QK_DIM).transpose(1, 0, 2, 3, 4, 5)

    def block(q_blk):
        s = jnp.einsum('bqhcd,bkhcd->bhcqk', q_blk, k).astype(jnp.float32) * scale
        p = jax.nn.softmax(s, axis=-1)
        attn = p[:, :, 0] - lam * p[:, :, 1]
        return jnp.einsum('bhqk,bkhv->bqhv', attn.astype(v.dtype), v)

    o = lax.map(block, qb)
    o = o.transpose(1, 0, 2, 3, 4).reshape(B, S, DIFF_HEADS, DIFF_V_DIM)
    o = rms_norm(o, subln_g) * (1.0 - lambda_init)
    return o.reshape(B, S, DIFF_WIDTH)


def wkv_scan(r, w, k, v, kk, a):
    xs = tuple(jnp.swapaxes(t, 0, 1) for t in (r, w, k, v, kk, a))
    B, _, H, N = r.shape
    s0 = jnp.zeros((B, H, N, N), r.dtype)

    def step(state, inp):
        r_t, w_t, k_t, v_t, kk_t, a_t = inp
        sa = jnp.einsum('bhij,bhj->bhi', state, -kk_t)
        state = (state * w_t[:, :, None, :]
                 + sa[:, :, :, None] * (kk_t * a_t)[:, :, None, :]
                 + v_t[:, :, :, None] * k_t[:, :, None, :])
        return state, jnp.einsum('bhij,bhj->bhi', state, r_t)

    _, y = lax.scan(step, s0, xs)
    return jnp.swapaxes(y, 0, 1)


def rwkv_time_mix(z, mu, w0, w2, a0, a2, g2, k_k, k_a, r_k, gn_w, gn_b):
    B, S = z.shape[0], z.shape[1]
    z = z + (0.5 * (shift_prev(z) + shift_next(z)) - z) * mu
    o1 = RWKV_WIDTH
    o2 = 2 * RWKV_WIDTH
    o3 = 3 * RWKV_WIDTH
    o4 = o3 + N_DIR * DECAY_LORA
    o5 = o4 + N_DIR * ICLR_LORA
    r, k, v = z[..., :o1], z[..., o1:o2], z[..., o2:o3]
    wd = z[..., o3:o4].reshape(B, S, N_DIR, DECAY_LORA)
    ad = z[..., o4:o5].reshape(B, S, N_DIR, ICLR_LORA)
    gd = z[..., o5:]

    def hs(t):
        return t.reshape(B, S, RWKV_HEADS, RWKV_HEAD_DIM)

    g = jax.nn.sigmoid(gd) @ g2
    kkf = hs(k * k_k).astype(jnp.float32)
    kk = (kkf * lax.rsqrt(jnp.maximum(jnp.sum(kkf * kkf, axis=-1, keepdims=True), 1e-12))).astype(k.dtype)
    rh, vh = hs(r), hs(v)
    y_sum = jnp.zeros_like(rh)
    bonus = jnp.zeros_like(rh)
    for d in range(N_DIR):
        w_log = -jax.nn.softplus(-(w0[d] + jnp.tanh(wd[:, :, d]) @ w2[d])) - 0.5
        decay = jnp.exp(-jnp.exp(w_log))
        a = jax.nn.sigmoid(a0[d] + ad[:, :, d] @ a2[d])
        kd = k * (1.0 + (a - 1.0) * k_a)
        args = (rh, hs(decay), hs(kd), vh, kk, hs(a))
        if d == 0:
            y = wkv_scan(*args)
        else:
            y = jnp.flip(wkv_scan(*[jnp.flip(t, axis=1) for t in args]), axis=1)
        y_sum = y_sum + y
        bonus = bonus + jnp.sum(rh * hs(kd) * r_k, axis=-1, keepdims=True) * vh
    yf = y_sum.astype(jnp.float32)
    mean = jnp.mean(yf, axis=-1, keepdims=True)
    var = jnp.mean(jnp.square(yf - mean), axis=-1, keepdims=True)
    yn = ((yf - mean) * lax.rsqrt(var + GN_EPS)).reshape(B, S, RWKV_WIDTH)
    yn = (yn * gn_w.astype(jnp.float32) + gn_b.astype(jnp.float32)).astype(z.dtype)
    return (yn + bonus.reshape(B, S, RWKV_WIDTH)) * g


def conv_ffn(z, w_up, conv_w, conv_b, w_down):
    gate, up = jnp.split(z @ w_up, 2, axis=-1)
    gate = conv_w[0] * shift_prev(gate) + conv_w[1] * gate + conv_w[2] * shift_next(gate) + conv_b
    return (jax.nn.gelu(gate, approximate=True) * up) @ w_down


def setup_inputs(seed: int = 0) -> dict:
    key = jax.random.key(seed)
    ks = jax.random.split(key, 32)
    L = DEPTH
    f32 = jnp.float32

    def nrm(k, shape, s):
        return jax.random.normal(k, shape, f32) * s

    return {
        "x": nrm(ks[0], (BATCH, SEQ, D_MODEL), 1.0),
        "pre_mix_norm": 1.0 + nrm(ks[1], (L, D_MODEL), 0.02),
        "post_mix_norm": 1.0 + nrm(ks[2], (L, D_MODEL), 0.02),
        "pre_ffn_norm": 1.0 + nrm(ks[3], (L, D_MODEL), 0.02),
        "post_ffn_norm": 1.0 + nrm(ks[4], (L, D_MODEL), 0.02),
        "w_in": nrm(ks[5], (L, D_MODEL, IN_COLS), D_MODEL ** -0.5),
        "diff_lambda_q1": nrm(ks[6], (L, DIFF_QK_DIM), 0.1),
        "diff_lambda_k1": nrm(ks[7], (L, DIFF_QK_DIM), 0.1),
        "diff_lambda_q2": nrm(ks[8], (L, DIFF_QK_DIM), 0.1),
        "diff_lambda_k2": nrm(ks[9], (L, DIFF_QK_DIM), 0.1),
        "diff_subln": 1.0 + nrm(ks[10], (L, DIFF_V_DIM), 0.02),
        "rwkv_mu": jax.random.uniform(ks[11], (L, RWKV_COLS), f32, 0.0, 1.0),
        "rwkv_w0": jax.random.uniform(ks[12], (L, N_DIR, RWKV_WIDTH), f32, -6.0, 1.0),
        "rwkv_w2": nrm(ks[13], (L, N_DIR, DECAY_LORA, RWKV_WIDTH), 0.5 * DECAY_LORA ** -0.5),
        "rwkv_a0": nrm(ks[14], (L, N_DIR, RWKV_WIDTH), 0.5),
        "rwkv_a2": nrm(ks[15], (L, N_DIR, ICLR_LORA, RWKV_WIDTH), 0.5 * ICLR_LORA ** -0.5),
        "rwkv_g2": nrm(ks[16], (L, GATE_LORA, RWKV_WIDTH), GATE_LORA ** -0.5),
        "rwkv_k_k": 0.85 + nrm(ks[17], (L, RWKV_WIDTH), 0.05),
        "rwkv_k_a": 1.0 + nrm(ks[18], (L, RWKV_WIDTH), 0.05),
        "rwkv_r_k": nrm(ks[19], (L, RWKV_HEADS, RWKV_HEAD_DIM), 0.1),
        "rwkv_gn_w": 1.0 + nrm(ks[20], (L, RWKV_WIDTH), 0.02),
        "rwkv_gn_b": nrm(ks[21], (L, RWKV_WIDTH), 0.02),
        "w_out": nrm(ks[22], (L, MIX_WIDTH, D_MODEL), MIX_WIDTH ** -0.5),
        "w_up": nrm(ks[23], (L, D_MODEL, 2 * D_FF), D_MODEL ** -0.5),
        "ffn_conv_w": nrm(ks[24], (L, CONV_WIDTH, D_FF), CONV_WIDTH ** -0.5),
        "ffn_conv_b": nrm(ks[25], (L, D_FF), 0.02),
        "w_down": nrm(ks[26], (L, D_FF, D_MODEL), D_FF ** -0.5),
    }


def reference(x, pre_mix_norm, post_mix_norm, pre_ffn_norm, post_ffn_norm, w_in,
              diff_lambda_q1, diff_lambda_k1, diff_lambda_q2, diff_lambda_k2, diff_subln,
              rwkv_mu, rwkv_w0, rwkv_w2, rwkv_a0, rwkv_a2, rwkv_g2, rwkv_k_k, rwkv_k_a,
              rwkv_r_k, rwkv_gn_w, rwkv_gn_b, w_out, w_up, ffn_conv_w, ffn_conv_b, w_down):
    for l in range(DEPTH):
        lambda_init = 0.8 - 0.6 * math.exp(-0.3 * l)
        h = rms_norm(x, pre_mix_norm[l])
        proj = h @ w_in[l]
        dq = proj[..., :DIFF_WIDTH]
        dk = proj[..., DIFF_WIDTH:2 * DIFF_WIDTH]
        dv = proj[..., 2 * DIFF_WIDTH:DIFF_COLS]
        rw = proj[..., DIFF_COLS:]
        ya = diff_attention(dq, dk, dv, diff_lambda_q1[l], diff_lambda_k1[l],
                            diff_lambda_q2[l], diff_lambda_k2[l], diff_subln[l], lambda_init)
        yb = rwkv_time_mix(rw, rwkv_mu[l], rwkv_w0[l], rwkv_w2[l], rwkv_a0[l], rwkv_a2[l],
                           rwkv_g2[l], rwkv_k_k[l], rwkv_k_a[l], rwkv_r_k[l],
                           rwkv_gn_w[l], rwkv_gn_b[l])
        mix = jnp.concatenate([ya, yb], axis=-1) @ w_out[l]
        x = x + rms_norm(mix, post_mix_norm[l])
        h = rms_norm(x, pre_ffn_norm[l])
        f = conv_ffn(h, w_up[l], ffn_conv_w[l], ffn_conv_b[l], w_down[l])
        x = x + rms_norm(f, post_ffn_norm[l])
    return x
```

```python
import functools
import math

import jax
import jax.numpy as jnp
from jax import lax
from jax.experimental import pallas as pl
from jax.experimental.pallas import tpu as pltpu

F32 = jnp.float32
BF16 = jnp.bfloat16

LANES = 128
SUBLANES = 8
VMEM_LIMIT = 56 * 1024 * 1024

DIFF_QK_DIM = 64
DIFF_V_DIM = 128
DIFF_HEADS = 4
DIFF_WIDTH = DIFF_HEADS * DIFF_V_DIM
RWKV_HEAD_DIM = 64
RWKV_WIDTH = 512
RWKV_PAIRS = RWKV_WIDTH // LANES
DECAY_LORA = 64
ICLR_LORA = 64
GATE_LORA = 128
ROPE_THETA = 10000.0
NORM_EPS = 1e-6
GN_EPS = 64e-5
CHUNK = 64

ROW_TILE = 512
ATTN_TQ = 256
ATTN_TK = 512
FF_CHUNK = 256


def _rms(x, g):
    return x * lax.rsqrt(jnp.mean(x * x, axis=-1, keepdims=True) + NORM_EPS) * g


def _mm(a, b):
    return jnp.dot(a.astype(BF16), b.astype(BF16), preferred_element_type=F32)


def _mm_nt(a, b):
    return lax.dot_general(a.astype(BF16), b.astype(BF16), (((1,), (1,)), ((), ())),
                           preferred_element_type=F32)


def _mm_tn(a, b):
    return lax.dot_general(a.astype(BF16), b.astype(BF16), (((0,), (0,)), ((), ())),
                           preferred_element_type=F32)


def _split2(x):
    hi = x.astype(BF16)
    lo = (x - hi.astype(F32)).astype(BF16)
    return hi, lo


def _split3(x):
    hi = x.astype(BF16)
    r1 = x - hi.astype(F32)
    mid = r1.astype(BF16)
    lo = (r1 - mid.astype(F32)).astype(BF16)
    return hi, mid, lo


def _seg_sum(x, bd):
    hi, lo = _split2(x)
    return (jnp.dot(hi, bd, preferred_element_type=F32)
            + jnp.dot(lo, bd, preferred_element_type=F32))


def _params(semantics):
    return pltpu.CompilerParams(dimension_semantics=semantics, vmem_limit_bytes=VMEM_LIMIT)


def _inproj_kernel(x_ref, g_ref, wqkv_ref, wrw_ref, cos_ref, sin_ref,
                   q0_ref, q1_ref, k_ref, v_ref, rw_ref):
    tm = x_ref.shape[0]
    hb = _rms(x_ref[...], g_ref[...]).astype(BF16)
    qk = jnp.dot(hb, wqkv_ref[:, :2 * DIFF_WIDTH], preferred_element_type=F32)
    cos = cos_ref[...]
    sin = sin_ref[...]
    lane = lax.broadcasted_iota(jnp.int32, (tm, LANES), 1)
    first_half = (lane % DIFF_QK_DIM) < (DIFF_QK_DIM // 2)
    comp0 = lane < DIFF_QK_DIM
    scale = DIFF_QK_DIM ** -0.5
    for j in range(2 * DIFF_HEADS):
        t = qk[:, j * LANES:(j + 1) * LANES]
        partner = jnp.where(first_half, pltpu.roll(t, LANES - DIFF_QK_DIM // 2, 1),
                            pltpu.roll(t, DIFF_QK_DIM // 2, 1))
        o = t * cos + partner * sin
        if j < DIFF_HEADS:
            o = o * scale
            q0_ref[:, j * LANES:(j + 1) * LANES] = jnp.where(comp0, o, 0.0).astype(BF16)
            q1_ref[:, j * LANES:(j + 1) * LANES] = jnp.where(comp0, 0.0, o).astype(BF16)
        else:
            jj = j - DIFF_HEADS
            k_ref[:, jj * LANES:(jj + 1) * LANES] = o.astype(BF16)
    v_ref[...] = jnp.dot(hb, wqkv_ref[:, 2 * DIFF_WIDTH:], preferred_element_type=F32).astype(BF16)
    rw_ref[...] = jnp.dot(hb, wrw_ref[...], preferred_element_type=F32)


def _in_proj(x2, g, wqkv, wrw, cos_t, sin_t, seq):
    m, d = x2.shape
    tm = ROW_TILE
    tps = seq // tm
    rw_cols = wrw.shape[1]
    row = lambda i: (i, 0)
    const = lambda i: (0, 0)
    return pl.pallas_call(
        _inproj_kernel,
        grid=(m // tm,),
        in_specs=[
            pl.BlockSpec((tm, d), row),
            pl.BlockSpec((1, d), const),
            pl.BlockSpec(wqkv.shape, const),
            pl.BlockSpec(wrw.shape, const),
            pl.BlockSpec((tm, LANES), lambda i: (i % tps, 0)),
            pl.BlockSpec((tm, LANES), lambda i: (i % tps, 0)),
        ],
        out_specs=[
            pl.BlockSpec((tm, DIFF_WIDTH), row),
            pl.BlockSpec((tm, DIFF_WIDTH), row),
            pl.BlockSpec((tm, DIFF_WIDTH), row),
            pl.BlockSpec((tm, DIFF_WIDTH), row),
            pl.BlockSpec((tm, rw_cols), row),
        ],
        out_shape=[
            jax.ShapeDtypeStruct((m, DIFF_WIDTH), BF16),
            jax.ShapeDtypeStruct((m, DIFF_WIDTH), BF16),
            jax.ShapeDtypeStruct((m, DIFF_WIDTH), BF16),
            jax.ShapeDtypeStruct((m, DIFF_WIDTH), BF16),
            jax.ShapeDtypeStruct((m, rw_cols), F32),
        ],
        compiler_params=_params(("parallel",)),
        name="in_proj",
    )(x2, g, wqkv, wrw, cos_t, sin_t)


def _attn_kernel(lam_ref, sub_ref, q0_ref, q1_ref, k_ref, v_ref, o_ref,
                 m_sc, l_sc, acc_sc, *, tk, lambda_init):
    seq = k_ref.shape[0]
    nk = seq // tk
    qs = (q0_ref[...], q1_ref[...])
    m_sc[...] = jnp.full(m_sc.shape, -jnp.inf, F32)
    l_sc[...] = jnp.zeros(l_sc.shape, F32)
    acc_sc[...] = jnp.zeros(acc_sc.shape, F32)

    def body(i, carry):
        start = pl.multiple_of(i * tk, tk)
        ks = k_ref[pl.ds(start, tk), :]
        vs = v_ref[pl.ds(start, tk), :]
        for c in range(2):
            s = lax.dot_general(qs[c], ks, (((1,), (1,)), ((), ())), preferred_element_type=F32)
            m_old = m_sc[c]
            m_new = jnp.maximum(m_old, jnp.max(s, axis=-1, keepdims=True))
            alpha = jnp.exp(m_old - m_new)
            p = jnp.exp(s - m_new)
            l_sc[c] = alpha * l_sc[c] + jnp.sum(p, axis=-1, keepdims=True)
            acc_sc[c] = alpha * acc_sc[c] + jnp.dot(p.astype(BF16), vs, preferred_element_type=F32)
            m_sc[c] = m_new
        return carry

    lax.fori_loop(0, nk, body, 0)

    lp = lam_ref[...]
    lam = (jnp.exp(jnp.sum(lp[0:1] * lp[1:2], axis=-1, keepdims=True))
           - jnp.exp(jnp.sum(lp[2:3] * lp[3:4], axis=-1, keepdims=True)) + lambda_init)
    o = acc_sc[0] / l_sc[0] - lam * (acc_sc[1] / l_sc[1])
    o = _rms(o, sub_ref[...]) * (1.0 - lambda_init)
    o_ref[...] = o.astype(BF16)


def _attention(lam_p, subln, q0, q1, k, v, batch, seq, lambda_init):
    tq, tk = ATTN_TQ, ATTN_TK
    nq = seq // tq
    qmap = lambda b, h, i: (b * nq + i, h)
    kvmap = lambda b, h, i: (b, h)
    const = lambda b, h, i: (0, 0)
    return pl.pallas_call(
        functools.partial(_attn_kernel, tk=tk, lambda_init=lambda_init),
        grid=(batch, DIFF_HEADS, nq),
        in_specs=[
            pl.BlockSpec(lam_p.shape, const),
            pl.BlockSpec(subln.shape, const),
            pl.BlockSpec((tq, LANES), qmap),
            pl.BlockSpec((tq, LANES), qmap),
            pl.BlockSpec((seq, LANES), kvmap),
            pl.BlockSpec((seq, LANES), kvmap),
        ],
        out_specs=pl.BlockSpec((tq, LANES), qmap),
        out_shape=jax.ShapeDtypeStruct((batch * seq, DIFF_WIDTH), BF16),
        scratch_shapes=[
            pltpu.VMEM((2, tq, 1), F32),
            pltpu.VMEM((2, tq, 1), F32),
            pltpu.VMEM((2, tq, DIFF_V_DIM), F32),
        ],
        compiler_params=_params(("parallel", "parallel", "parallel")),
        name="diff_attn",
    )(lam_p, subln, q0, q1, k, v)


def _sigmoid(x):
    return 1.0 / (1.0 + jnp.exp(-x))


def _rwkv_prep_kernel(z_ref, zp_ref, zn_ref, mu_ref, w0_ref, w2_ref, a0_ref, a2_ref, g2_ref,
                      kk_ref, ka_ref, rk_ref, bd_ref, tri_ref, sel_ref,
                      v_o, g_o, bonus_o,
                      at0_o, rt0_o, bb0_o, kb0_o, bh0_o, kh0_o, pl0_o,
                      at1_o, rt1_o, bb1_o, kb1_o, bh1_o, kh1_o, pl1_o, *, tiles_per_seq):
    tm = z_ref.shape[0]
    w = RWKV_WIDTH
    i = pl.program_id(0)
    first = (i % tiles_per_seq) == 0
    last = (i % tiles_per_seq) == tiles_per_seq - 1
    z = z_ref[...]
    prow = jnp.where(first, 0.0, zp_ref[SUBLANES - 1:SUBLANES, :])
    nrow = jnp.where(last, 0.0, zn_ref[0:1, :])
    row = lax.broadcasted_iota(jnp.int32, (tm, 1), 0)
    zp = jnp.where(row == 0, prow, pltpu.roll(z, 1, 0))
    zn = jnp.where(row == tm - 1, nrow, pltpu.roll(z, tm - 1, 0))
    zs = z + (0.5 * (zp + zn) - z) * mu_ref[...]

    r = zs[:, 0:w]
    k = zs[:, w:2 * w]
    v = zs[:, 2 * w:3 * w]
    o3 = 3 * w
    wd = zs[:, o3:o3 + 2 * DECAY_LORA]
    ad = zs[:, o3 + 2 * DECAY_LORA:o3 + 2 * DECAY_LORA + 2 * ICLR_LORA]
    gd = zs[:, o3 + 2 * DECAY_LORA + 2 * ICLR_LORA:]
    bd = bd_ref[...]

    g_o[...] = _mm(_sigmoid(gd), g2_ref[...])
    v_o[...] = v.astype(BF16)
    kkf = k * kk_ref[...]
    kk = kkf * lax.rsqrt(jnp.maximum(_seg_sum(kkf * kkf, bd), 1e-12))
    wl_all = _mm(jnp.tanh(wd), w2_ref[...])
    al_all = _mm(ad, a2_ref[...])
    outs = ((at0_o, rt0_o, bb0_o, kb0_o, bh0_o, kh0_o, pl0_o),
            (at1_o, rt1_o, bb1_o, kb1_o, bh1_o, kh1_o, pl1_o))
    bonus = jnp.zeros((tm, w), F32)
    for d in range(2):
        at_o, rt_o, bb_o, kb_o, bh_o, kh_o, pl_o = outs[d]
        x = -(w0_ref[d:d + 1, :] + wl_all[:, d * w:(d + 1) * w])
        softplus = jnp.maximum(x, 0.0) + jnp.log(1.0 + jnp.exp(-jnp.abs(x)))
        lw = -jnp.exp(-softplus - 0.5)
        a = _sigmoid(a0_ref[d:d + 1, :] + al_all[:, d * w:(d + 1) * w])
        kd = k * (1.0 + (a - 1.0) * ka_ref[...])
        bonus = bonus + _seg_sum(r * kd * rk_ref[...], bd) * v
        h3 = _split3(lw)
        cin = sum(jnp.dot(tri_ref[2 * d], p, preferred_element_type=F32) for p in h3)
        rem = sum(jnp.dot(tri_ref[2 * d + 1], p, preferred_element_type=F32) for p in h3)
        tot = sum(jnp.dot(sel_ref[...], p, preferred_element_type=F32) for p in h3)
        e_in = jnp.exp(cin)
        e_neg = jnp.exp(-cin)
        e_rem = jnp.exp(rem)
        beta = kk * a
        at_o[...] = (-kk * jnp.exp(cin - lw)).astype(BF16)
        rt_o[...] = (r * e_in).astype(BF16)
        bb_o[...] = (beta * e_neg).astype(BF16)
        kb_o[...] = (kd * e_neg).astype(BF16)
        bh_o[...] = (beta * e_rem).astype(BF16)
        kh_o[...] = (kd * e_rem).astype(BF16)
        pl_o[...] = jnp.exp(tot)
    bonus_o[...] = bonus


def _rwkv_prep(rw, mu, w0, w2big, a0, a2big, g2, k_k, k_a, r_k, bd, tri, sel, seq):
    m, cols = rw.shape
    tm = ROW_TILE
    nblk8 = m // SUBLANES
    w = RWKV_WIDTH
    row = lambda i: (i, 0)
    const2 = lambda i: (0, 0)
    const3 = lambda i: (0, 0, 0)
    full = lambda a: pl.BlockSpec(a.shape, const2 if a.ndim == 2 else const3)
    big_bf = jax.ShapeDtypeStruct((m, w), BF16)
    big_f = jax.ShapeDtypeStruct((m, w), F32)
    pl_shape = jax.ShapeDtypeStruct((m // CHUNK, w), F32)
    per_dir_shapes = [big_bf] * 6 + [pl_shape]
    per_dir_specs = [pl.BlockSpec((tm, w), row)] * 6 + [pl.BlockSpec((tm // CHUNK, w), row)]
    return pl.pallas_call(
        functools.partial(_rwkv_prep_kernel, tiles_per_seq=seq // tm),
        grid=(m // tm,),
        in_specs=[
            pl.BlockSpec((tm, cols), row),
            pl.BlockSpec((SUBLANES, cols), lambda i: (jnp.maximum(i * (tm // SUBLANES) - 1, 0), 0)),
            pl.BlockSpec((SUBLANES, cols), lambda i: (jnp.minimum((i + 1) * (tm // SUBLANES), nblk8 - 1), 0)),
            full(mu), full(w0), full(w2big), full(a0), full(a2big), full(g2),
            full(k_k), full(k_a), full(r_k), full(bd), full(tri), full(sel),
        ],
        out_specs=[pl.BlockSpec((tm, w), row)] * 3 + per_dir_specs * 2,
        out_shape=[big_bf, big_f, big_f] + per_dir_shapes * 2,
        compiler_params=_params(("parallel",)),
        name="rwkv_prep",
    )(rw, rw, rw, mu, w0, w2big, a0, a2big, g2, k_k, k_a, r_k, bd, tri, sel)


def _wkv_kernel(v0_ref, at0, rt0, bb0, kb0, bh0, kh0, pl0,
                v1_ref, at1, rt1, bb1, kb1, bh1, kh1, pl1,
                y0_ref, y1_ref, s_sc):
    c = pl.program_id(1)

    @pl.when(c == 0)
    def _():
        s_sc[...] = jnp.zeros(s_sc.shape, F32)

    n = 2 * CHUNK
    ri = lax.broadcasted_iota(jnp.int32, (n, n), 0)
    ci = lax.broadcasted_iota(jnp.int32, (n, n), 1)
    ti, tj = ri % CHUNK, ci % CHUNK
    same = (ri // CHUNK) == (ci // CHUNK)
    eye = ri == ci
    head0 = lax.broadcasted_iota(jnp.int32, (CHUNK, LANES), 1) < RWKV_HEAD_DIM

    def stack(x):
        zero = jnp.zeros_like(x)
        return jnp.concatenate([jnp.where(head0, x, zero), jnp.where(head0, zero, x)], axis=0)

    dirs = ((v0_ref, at0, rt0, bb0, kb0, bh0, kh0, pl0, y0_ref),
            (v1_ref, at1, rt1, bb1, kb1, bh1, kh1, pl1, y1_ref))
    for d, (v_ref, at_r, rt_r, bb_r, kb_r, bh_r, kh_r, pl_r, y_ref) in enumerate(dirs):
        before = (ti > tj) if d == 0 else (ti < tj)
        strict = same & before
        incl = strict | eye
        levels = []
        b = 1
        while b < CHUNK:
            blk = same & ((ti // (2 * b)) == (tj // (2 * b)))
            hi, hj = (ti // b) % 2, (tj // b) % 2
            levels.append(blk & ((hi == 1) & (hj == 0) if d == 0 else (hi == 0) & (hj == 1)))
            b *= 2
        for p in range(RWKV_PAIRS):
            sl = slice(p * LANES, (p + 1) * LANES)
            at_s, rt_s, bb_s, kb_s = stack(at_r[:, sl]), stack(rt_r[:, sl]), stack(bb_r[:, sl]), stack(kb_r[:, sl])
            bh_s, kh_s, v_s = stack(bh_r[:, sl]), stack(kh_r[:, sl]), stack(v_ref[:, sl])
            a_ab = jnp.where(strict, _mm_nt(at_s, bb_s), 0.0)
            a_ak = jnp.where(strict, _mm_nt(at_s, kb_s), 0.0)
            a_rb = jnp.where(incl, _mm_nt(rt_s, bb_s), 0.0)
            a_rk = jnp.where(incl, _mm_nt(rt_s, kb_s), 0.0)
            t_inv = jnp.where(eye, 1.0, jnp.where(levels[0], a_ab, 0.0))
            for lm in levels[1:]:
                t_inv = t_inv + _mm(_mm(t_inv, jnp.where(lm, a_ab, 0.0)), t_inv)
            akv = _mm(a_ak, v_s)
            wu = _mm(t_inv, jnp.concatenate([at_s, akv.astype(BF16)], axis=1))
            qy = _mm(a_rb, wu)
            qt = rt_s.astype(F32) + qy[:, :LANES]
            y0 = qy[:, LANES:] + _mm(a_rk, v_s)
            mn = _mm_tn(wu, bh_s)
            n0t = mn[LANES:, :] + _mm_tn(v_s, kh_s)
            st = s_sc[d, p]
            ys = _mm_nt(qt, st) + y0
            y_ref[:, sl] = ys[:CHUNK] + ys[CHUNK:]
            s_sc[d, p] = st * pl_r[0, :, sl] + _mm(st, mn[:LANES, :]) + n0t


def _wkv(v, dir0, dir1, batch, seq):
    m, w = v.shape
    nc = seq // CHUNK
    fmap = lambda b, c: (b * nc + c, 0)
    bmap = lambda b, c: (b * nc + nc - 1 - c, 0)
    fmap3 = lambda b, c: (b * nc + c, 0, 0)
    bmap3 = lambda b, c: (b * nc + nc - 1 - c, 0, 0)
    blk = lambda im: pl.BlockSpec((CHUNK, w), im)
    in_specs = ([blk(fmap)] * 7 + [pl.BlockSpec((1, 1, w), fmap3)]
                + [blk(bmap)] * 7 + [pl.BlockSpec((1, 1, w), bmap3)])
    return pl.pallas_call(
        _wkv_kernel,
        grid=(batch, nc),
        in_specs=in_specs,
        out_specs=[blk(fmap), blk(bmap)],
        out_shape=[jax.ShapeDtypeStruct((m, w), F32)] * 2,
        scratch_shapes=[pltpu.VMEM((2, RWKV_PAIRS, LANES, LANES), F32)],
        compiler_params=_params(("parallel", "arbitrary")),
        name="wkv_scan",
    )(v, *dir0, v, *dir1)


def _outproj_kernel(x_ref, ya_ref, yf_ref, yb_ref, bonus_ref, g_ref, gnw_ref, gnb_ref, bd_ref,
                    wout_ref, pg_ref, o_ref):
    bd = bd_ref[...]
    inv_n = 1.0 / RWKV_HEAD_DIM
    y = yf_ref[...] + yb_ref[...]
    mean = _seg_sum(y, bd) * inv_n
    yc = y - mean
    var = _seg_sum(yc * yc, bd) * inv_n
    yn = yc * lax.rsqrt(var + GN_EPS) * gnw_ref[...] + gnb_ref[...]
    yb = ((yn + bonus_ref[...]) * g_ref[...]).astype(BF16)
    mix = (jnp.dot(ya_ref[...], wout_ref[:DIFF_WIDTH, :], preferred_element_type=F32)
           + jnp.dot(yb, wout_ref[DIFF_WIDTH:, :], preferred_element_type=F32))
    o_ref[...] = x_ref[...] + _rms(mix, pg_ref[...])


def _out_proj(x2, ya, yf, yb, bonus, g, gn_w, gn_b, bd, wout, post_g):
    m, d = x2.shape
    tm = ROW_TILE
    w = RWKV_WIDTH
    row = lambda i: (i, 0)
    const = lambda i: (0, 0)
    full = lambda a: pl.BlockSpec(a.shape, const)
    return pl.pallas_call(
        _outproj_kernel,
        grid=(m // tm,),
        in_specs=[pl.BlockSpec((tm, d), row), pl.BlockSpec((tm, DIFF_WIDTH), row)]
                 + [pl.BlockSpec((tm, w), row)] * 4
                 + [full(gn_w), full(gn_b), full(bd), full(wout), full(post_g)],
        out_specs=pl.BlockSpec((tm, d), row),
        out_shape=jax.ShapeDtypeStruct((m, d), F32),
        compiler_params=_params(("parallel",)),
        name="out_proj",
    )(x2, ya, yf, yb, bonus, g, gn_w, gn_b, bd, wout, post_g)


def _gelu_tanh(x):
    c = math.sqrt(2.0 / math.pi)
    return 0.5 * x * (1.0 + jnp.tanh(c * (x + 0.044715 * (x * x * x))))


def _ffn_kernel(x_ref, xp_ref, xn_ref, g_ref, wup_ref, cw_ref, cb_ref, wdn_ref, pg_ref, o_ref,
                *, tiles_per_seq, d_ff):
    tm = x_ref.shape[0]
    i = pl.program_id(0)
    first = (i % tiles_per_seq) == 0
    last = (i % tiles_per_seq) == tiles_per_seq - 1
    g = g_ref[...]
    x = x_ref[...]
    hb = _rms(x, g).astype(BF16)
    hp = _rms(xp_ref[...], g).astype(BF16)
    hn = _rms(xn_ref[...], g).astype(BF16)
    hext = jnp.concatenate([hp, hb, hn], axis=0)
    ext = tm + 2 * SUBLANES
    row = lax.broadcasted_iota(jnp.int32, (ext, 1), 0)
    kill_prev = jnp.logical_and(first, row == SUBLANES)
    kill_next = jnp.logical_and(last, row == SUBLANES + tm - 1)
    acc = jnp.zeros((tm, o_ref.shape[1]), F32)
    for j in range(d_ff // FF_CHUNK):
        cs = slice(j * FF_CHUNK, (j + 1) * FF_CHUNK)
        gate = jnp.dot(hext, wup_ref[:, cs], preferred_element_type=F32)
        up = jnp.dot(hb, wup_ref[:, d_ff + j * FF_CHUNK:d_ff + (j + 1) * FF_CHUNK],
                     preferred_element_type=F32)
        gp = jnp.where(kill_prev, 0.0, pltpu.roll(gate, 1, 0))
        gn = jnp.where(kill_next, 0.0, pltpu.roll(gate, ext - 1, 0))
        conv = cw_ref[0:1, cs] * gp + cw_ref[1:2, cs] * gate + cw_ref[2:3, cs] * gn + cb_ref[:, cs]
        conv = conv[SUBLANES:SUBLANES + tm, :]
        act = (_gelu_tanh(conv) * up).astype(BF16)
        acc = acc + jnp.dot(act, wdn_ref[cs, :], preferred_element_type=F32)
    o_ref[...] = x + _rms(acc, pg_ref[...])


def _ffn(x1, pre_g, wup, conv_w, conv_b, wdn, post_g, seq):
    m, d = x1.shape
    tm = ROW_TILE
    d_ff = wdn.shape[0]
    nblk8 = m // SUBLANES
    row = lambda i: (i, 0)
    const = lambda i: (0, 0)
    full = lambda a: pl.BlockSpec(a.shape, const)
    return pl.pallas_call(
        functools.partial(_ffn_kernel, tiles_per_seq=seq // tm, d_ff=d_ff),
        grid=(m // tm,),
        in_specs=[
            pl.BlockSpec((tm, d), row),
            pl.BlockSpec((SUBLANES, d), lambda i: (jnp.maximum(i * (tm // SUBLANES) - 1, 0), 0)),
            pl.BlockSpec((SUBLANES, d), lambda i: (jnp.minimum((i + 1) * (tm // SUBLANES), nblk8 - 1), 0)),
            full(pre_g), full(wup), full(conv_w), full(conv_b), full(wdn), full(post_g),
        ],
        out_specs=pl.BlockSpec((tm, d), row),
        out_shape=jax.ShapeDtypeStruct((m, d), F32),
        compiler_params=_params(("parallel",)),
        name="conv_ffn",
    )(x1, x1, x1, pre_g, wup, conv_w, conv_b, wdn, post_g)


def _rope_tables(seq):
    half = DIFF_QK_DIM // 2
    pos = jnp.arange(seq, dtype=F32)
    inv = ROPE_THETA ** (-jnp.arange(half, dtype=F32) / half)
    ang = pos[:, None] * inv[None, :]
    cos, sin = jnp.cos(ang), jnp.sin(ang)
    reps = LANES // DIFF_QK_DIM
    cos_t = jnp.tile(jnp.concatenate([cos, cos], axis=-1), (1, reps))
    sin_t = jnp.tile(jnp.concatenate([-sin, sin], axis=-1), (1, reps))
    return cos_t, sin_t


def _scan_constants(tm):
    t = jnp.arange(tm)
    same = (t[:, None] // CHUNK) == (t[None, :] // CHUNK)
    lower = t[:, None] >= t[None, :]
    upper = t[:, None] <= t[None, :]
    tri = jnp.stack([same & lower, same & ~lower,
                     same & upper, same & ~upper])
    sel = (jnp.arange(tm // CHUNK)[:, None] == (t[None, :] // CHUNK))
    ch = jnp.arange(RWKV_WIDTH)
    bd = (ch[:, None] // RWKV_HEAD_DIM) == (ch[None, :] // RWKV_HEAD_DIM)
    return tri.astype(BF16), sel.astype(BF16), bd.astype(BF16)


def _lora_blockdiag(w2):
    z = jnp.zeros_like(w2[0])
    return jnp.concatenate([jnp.concatenate([w2[0], z], axis=1),
                            jnp.concatenate([z, w2[1]], axis=1)], axis=0).astype(BF16)


def kernel(x, pre_mix_norm, post_mix_norm, pre_ffn_norm, post_ffn_norm, w_in, diff_lambda_q1, diff_lambda_k1, diff_lambda_q2, diff_lambda_k2, diff_subln, rwkv_mu, rwkv_w0, rwkv_w2, rwkv_a0, rwkv_a2, rwkv_g2, rwkv_k_k, rwkv_k_a, rwkv_r_k, rwkv_gn_w, rwkv_gn_b, w_out, w_up, ffn_conv_w, ffn_conv_b, w_down):
    batch, seq, d = x.shape
    depth = w_in.shape[0]
    assert seq % ROW_TILE == 0 and seq % ATTN_TK == 0 and ROW_TILE % CHUNK == 0
    m = batch * seq
    nc_total = m // CHUNK
    cos_t, sin_t = _rope_tables(seq)
    tri, sel, bd = _scan_constants(ROW_TILE)
    x2 = x.reshape(m, d)
    diff_cols = 3 * DIFF_WIDTH
    row2 = lambda a: a.reshape(1, -1)
    for l in range(depth):
        lambda_init = 0.8 - 0.6 * math.exp(-0.3 * l)
        wqkv = w_in[l][:, :diff_cols].astype(BF16)
        wrw = w_in[l][:, diff_cols:].astype(BF16)
        q0, q1, k, v, rw = _in_proj(x2, row2(pre_mix_norm[l]), wqkv, wrw, cos_t, sin_t, seq)
        lam_p = jnp.stack([diff_lambda_q1[l], diff_lambda_k1[l], diff_lambda_q2[l], diff_lambda_k2[l]])
        ya = _attention(lam_p, row2(diff_subln[l]), q0, q1, k, v, batch, seq, lambda_init)
        prep = _rwkv_prep(rw, row2(rwkv_mu[l]), rwkv_w0[l], _lora_blockdiag(rwkv_w2[l]), rwkv_a0[l],
                          _lora_blockdiag(rwkv_a2[l]), rwkv_g2[l].astype(BF16), row2(rwkv_k_k[l]),
                          row2(rwkv_k_a[l]), row2(rwkv_r_k[l]), bd, tri, sel, seq)
        vb, g, bonus = prep[0], prep[1], prep[2]
        dir0 = list(prep[3:9]) + [prep[9].reshape(nc_total, 1, RWKV_WIDTH)]
        dir1 = list(prep[10:16]) + [prep[16].reshape(nc_total, 1, RWKV_WIDTH)]
        yf, yb = _wkv(vb, dir0, dir1, batch, seq)
        x2 = _out_proj(x2, ya, yf, yb, bonus, g, row2(rwkv_gn_w[l]), row2(rwkv_gn_b[l]), bd,
                       w_out[l].astype(BF16), row2(post_mix_norm[l]))
        x2 = _ffn(x2, row2(pre_ffn_norm[l]), w_up[l].astype(BF16), ffn_conv_w[l], row2(ffn_conv_b[l]),
                  w_down[l].astype(BF16), row2(post_ffn_norm[l]), seq)
    return x2.reshape(batch, seq, d)
```

```python
import functools
import math

import jax
import jax.numpy as jnp
from jax import lax
from jax.experimental import pallas as pl
from jax.experimental.pallas import tpu as pltpu

F32 = jnp.float32
BF16 = jnp.bfloat16

LANES = 128
SUBLANES = 8
VMEM_LIMIT = 56 * 1024 * 1024

DIFF_QK_DIM = 64
DIFF_V_DIM = 128
DIFF_HEADS = 4
DIFF_WIDTH = DIFF_HEADS * DIFF_V_DIM
RWKV_HEAD_DIM = 64
RWKV_WIDTH = 512
RWKV_PAIRS = RWKV_WIDTH // LANES
DECAY_LORA = 64
ICLR_LORA = 64
GATE_LORA = 128
ROPE_THETA = 10000.0
NORM_EPS = 1e-6
GN_EPS = 64e-5
CHUNK = 64

ROW_TILE = 512
ATTN_TQ = 256
ATTN_TK = 1024
ATTN_VROWS = 144
FF_CHUNK = 256


def _rms(x, g):
    return x * lax.rsqrt(jnp.mean(x * x, axis=-1, keepdims=True) + NORM_EPS) * g


def _mm(a, b):
    return jnp.dot(a.astype(BF16), b.astype(BF16), preferred_element_type=F32)


def _mm_nt(a, b):
    return lax.dot_general(a.astype(BF16), b.astype(BF16), (((1,), (1,)), ((), ())),
                           preferred_element_type=F32)


def _mm_tn(a, b):
    return lax.dot_general(a.astype(BF16), b.astype(BF16), (((0,), (0,)), ((), ())),
                           preferred_element_type=F32)


def _split2(x):
    hi = x.astype(BF16)
    lo = (x - hi.astype(F32)).astype(BF16)
    return hi, lo


def _split3(x):
    hi = x.astype(BF16)
    r1 = x - hi.astype(F32)
    mid = r1.astype(BF16)
    lo = (r1 - mid.astype(F32)).astype(BF16)
    return hi, mid, lo


def _seg_sum(x, bd):
    hi, lo = _split2(x)
    return (jnp.dot(hi, bd, preferred_element_type=F32)
            + jnp.dot(lo, bd, preferred_element_type=F32))


def _params(semantics):
    return pltpu.CompilerParams(dimension_semantics=semantics, vmem_limit_bytes=VMEM_LIMIT)


def _inproj_kernel(x_ref, g_ref, wqkv_ref, wrw_ref, cos_ref, sin_ref,
                   q0_ref, q1_ref, k_ref, v_ref, rw_ref):
    tm = x_ref.shape[0]
    hb = _rms(x_ref[...], g_ref[...]).astype(BF16)
    qk = jnp.dot(hb, wqkv_ref[:, :2 * DIFF_WIDTH], preferred_element_type=F32)
    cos = cos_ref[...]
    sin = sin_ref[...]
    lane = lax.broadcasted_iota(jnp.int32, (tm, LANES), 1)
    first_half = (lane % DIFF_QK_DIM) < (DIFF_QK_DIM // 2)
    comp0 = lane < DIFF_QK_DIM
    scale = DIFF_QK_DIM ** -0.5
    for j in range(2 * DIFF_HEADS):
        t = qk[:, j * LANES:(j + 1) * LANES]
        partner = jnp.where(first_half, pltpu.roll(t, LANES - DIFF_QK_DIM // 2, 1),
                            pltpu.roll(t, DIFF_QK_DIM // 2, 1))
        o = t * cos + partner * sin
        if j < DIFF_HEADS:
            o = o * scale
            q0_ref[:, j * LANES:(j + 1) * LANES] = jnp.where(comp0, o, 0.0).astype(BF16)
            q1_ref[:, j * LANES:(j + 1) * LANES] = jnp.where(comp0, 0.0, o).astype(BF16)
        else:
            jj = j - DIFF_HEADS
            k_ref[:, jj * LANES:(jj + 1) * LANES] = o.astype(BF16)
    v_ref[...] = jnp.dot(hb, wqkv_ref[:, 2 * DIFF_WIDTH:], preferred_element_type=F32).astype(BF16)
    rw_ref[...] = jnp.dot(hb, wrw_ref[...], preferred_element_type=F32)


def _in_proj(x2, g, wqkv, wrw, cos_t, sin_t, seq):
    m, d = x2.shape
    tm = ROW_TILE
    tps = seq // tm
    rw_cols = wrw.shape[1]
    row = lambda i: (i, 0)
    const = lambda i: (0, 0)
    return pl.pallas_call(
        _inproj_kernel,
        grid=(m // tm,),
        in_specs=[
            pl.BlockSpec((tm, d), row),
            pl.BlockSpec((1, d), const),
            pl.BlockSpec(wqkv.shape, const),
            pl.BlockSpec(wrw.shape, const),
            pl.BlockSpec((tm, LANES), lambda i: (i % tps, 0)),
            pl.BlockSpec((tm, LANES), lambda i: (i % tps, 0)),
        ],
        out_specs=[
            pl.BlockSpec((tm, DIFF_WIDTH), row),
            pl.BlockSpec((tm, DIFF_WIDTH), row),
            pl.BlockSpec((tm, DIFF_WIDTH), row),
            pl.BlockSpec((tm, DIFF_WIDTH), row),
            pl.BlockSpec((tm, rw_cols), row),
        ],
        out_shape=[
            jax.ShapeDtypeStruct((m, DIFF_WIDTH), BF16),
            jax.ShapeDtypeStruct((m, DIFF_WIDTH), BF16),
            jax.ShapeDtypeStruct((m, DIFF_WIDTH), BF16),
            jax.ShapeDtypeStruct((m, DIFF_WIDTH), BF16),
            jax.ShapeDtypeStruct((m, rw_cols), F32),
        ],
        compiler_params=_params(("parallel",)),
        name="in_proj",
    )(x2, g, wqkv, wrw, cos_t, sin_t)


def _attn_kernel(lam_ref, sub_ref, q0_ref, q1_ref, k_ref, vt_ref, o_ref, s_sc, m_sc, acc_sc,
                 *, tk, lambda_init):
    seq = k_ref.shape[0]
    nk = seq // tk
    qs = (q0_ref[...], q1_ref[...])
    m_sc[...] = jnp.full(m_sc.shape, -jnp.inf, F32)
    acc_sc[...] = jnp.zeros(acc_sc.shape, F32)

    def scores(i, slot):
        start = pl.multiple_of(i * tk, tk)
        ks = k_ref[pl.ds(start, tk), :]
        maxima = []
        for c in range(2):
            st = lax.dot_general(ks, qs[c], (((1,), (1,)), ((), ())), preferred_element_type=F32)
            s_sc[slot, c] = st
            maxima.append(jnp.max(st, axis=0, keepdims=True))
        return tuple(maxima)

    def accumulate(i, slot, maxima):
        start = pl.multiple_of(i * tk, tk)
        vt = vt_ref[:, pl.ds(start, tk)]
        for c in range(2):
            m_old = m_sc[c]
            m_new = jnp.maximum(m_old, maxima[c])
            alpha = jnp.exp(m_old - m_new)
            p = jnp.exp(s_sc[slot, c] - m_new).astype(BF16)
            acc_sc[c] = alpha * acc_sc[c] + jnp.dot(vt, p, preferred_element_type=F32)
            m_sc[c] = m_new

    def body(j, maxima):
        nxt = scores(2 * j + 1, 1)
        accumulate(2 * j, 0, maxima)
        maxima = scores(2 * j + 2, 0)
        accumulate(2 * j + 1, 1, nxt)
        return maxima

    maxima = lax.fori_loop(0, (nk - 2) // 2, body, scores(0, 0))
    nxt = scores(nk - 1, 1)
    accumulate(nk - 2, 0, maxima)
    accumulate(nk - 1, 1, nxt)

    lp = lam_ref[...]
    lam = (jnp.exp(jnp.sum(lp[0:1] * lp[1:2], axis=-1, keepdims=True))
           - jnp.exp(jnp.sum(lp[2:3] * lp[3:4], axis=-1, keepdims=True)) + lambda_init)
    a0 = acc_sc[0]
    a1 = acc_sc[1]
    dv = DIFF_V_DIM
    o = a0[:dv] / a0[dv:dv + 1] - lam * (a1[:dv] / a1[dv:dv + 1])
    o = o * lax.rsqrt(jnp.mean(o * o, axis=0, keepdims=True) + NORM_EPS) * sub_ref[...] * (1.0 - lambda_init)
    o_ref[...] = o.T.astype(BF16)


def _attention(lam_p, subln_col, q0, q1, k, vt, batch, seq, lambda_init):
    tq, tk = ATTN_TQ, min(ATTN_TK, seq // 2)
    assert (seq // tk) % 2 == 0
    nq = seq // tq
    qmap = lambda b, h, i: (b * nq + i, h)
    const = lambda b, h, i: (0, 0)
    return pl.pallas_call(
        functools.partial(_attn_kernel, tk=tk, lambda_init=lambda_init),
        grid=(batch, DIFF_HEADS, nq),
        in_specs=[
            pl.BlockSpec(lam_p.shape, const),
            pl.BlockSpec(subln_col.shape, const),
            pl.BlockSpec((tq, LANES), qmap),
            pl.BlockSpec((tq, LANES), qmap),
            pl.BlockSpec((seq, LANES), lambda b, h, i: (b, h)),
            pl.BlockSpec((ATTN_VROWS, seq), lambda b, h, i: (b * DIFF_HEADS + h, 0)),
        ],
        out_specs=pl.BlockSpec((tq, LANES), qmap),
        out_shape=jax.ShapeDtypeStruct((batch * seq, DIFF_WIDTH), BF16),
        scratch_shapes=[
            pltpu.VMEM((2, 2, tk, tq), F32),
            pltpu.VMEM((2, 1, tq), F32),
            pltpu.VMEM((2, ATTN_VROWS, tq), F32),
        ],
        compiler_params=_params(("parallel", "parallel", "parallel")),
        name="diff_attn",
    )(lam_p, subln_col, q0, q1, k, vt)


def _values_transposed(v, batch, seq):
    vt = v.reshape(batch, seq, DIFF_HEADS, DIFF_V_DIM).transpose(0, 2, 3, 1)
    pad = jnp.zeros((batch, DIFF_HEADS, ATTN_VROWS - DIFF_V_DIM, seq), v.dtype).at[:, :, 0, :].set(1)
    return jnp.concatenate([vt, pad], axis=2).reshape(batch * DIFF_HEADS * ATTN_VROWS, seq)


def _sigmoid(x):
    return 1.0 / (1.0 + jnp.exp(-x))


def _rwkv_prep_kernel(z_ref, zp_ref, zn_ref, mu_ref, w0_ref, w2_ref, a0_ref, a2_ref, g2_ref,
                      kk_ref, ka_ref, rk_ref, bd_ref, tri_ref, sel_ref,
                      v_o, g_o, bonus_o,
                      at0_o, rt0_o, bb0_o, kb0_o, bh0_o, kh0_o, pl0_o,
                      at1_o, rt1_o, bb1_o, kb1_o, bh1_o, kh1_o, pl1_o, *, tiles_per_seq):
    tm = z_ref.shape[0]
    w = RWKV_WIDTH
    i = pl.program_id(0)
    first = (i % tiles_per_seq) == 0
    last = (i % tiles_per_seq) == tiles_per_seq - 1
    z = z_ref[...]
    prow = jnp.where(first, 0.0, zp_ref[SUBLANES - 1:SUBLANES, :])
    nrow = jnp.where(last, 0.0, zn_ref[0:1, :])
    row = lax.broadcasted_iota(jnp.int32, (tm, 1), 0)
    zp = jnp.where(row == 0, prow, pltpu.roll(z, 1, 0))
    zn = jnp.where(row == tm - 1, nrow, pltpu.roll(z, tm - 1, 0))
    zs = z + (0.5 * (zp + zn) - z) * mu_ref[...]

    r = zs[:, 0:w]
    k = zs[:, w:2 * w]
    v = zs[:, 2 * w:3 * w]
    o3 = 3 * w
    wd = zs[:, o3:o3 + 2 * DECAY_LORA]
    ad = zs[:, o3 + 2 * DECAY_LORA:o3 + 2 * DECAY_LORA + 2 * ICLR_LORA]
    gd = zs[:, o3 + 2 * DECAY_LORA + 2 * ICLR_LORA:]
    bd = bd_ref[...]

    g_o[...] = _mm(_sigmoid(gd), g2_ref[...])
    v_o[...] = v.astype(BF16)
    kkf = k * kk_ref[...]
    kk = kkf * lax.rsqrt(jnp.maximum(_seg_sum(kkf * kkf, bd), 1e-12))
    wl_all = _mm(jnp.tanh(wd), w2_ref[...])
    al_all = _mm(ad, a2_ref[...])
    outs = ((at0_o, rt0_o, bb0_o, kb0_o, bh0_o, kh0_o, pl0_o),
            (at1_o, rt1_o, bb1_o, kb1_o, bh1_o, kh1_o, pl1_o))
    bonus = jnp.zeros((tm, w), F32)
    for d in range(2):
        at_o, rt_o, bb_o, kb_o, bh_o, kh_o, pl_o = outs[d]
        x = -(w0_ref[d:d + 1, :] + wl_all[:, d * w:(d + 1) * w])
        softplus = jnp.maximum(x, 0.0) + jnp.log(1.0 + jnp.exp(-jnp.abs(x)))
        lw = -jnp.exp(-softplus - 0.5)
        a = _sigmoid(a0_ref[d:d + 1, :] + al_all[:, d * w:(d + 1) * w])
        kd = k * (1.0 + (a - 1.0) * ka_ref[...])
        bonus = bonus + _seg_sum(r * kd * rk_ref[...], bd) * v
        h3 = _split3(lw)
        cin = sum(jnp.dot(tri_ref[2 * d], p, preferred_element_type=F32) for p in h3)
        rem = sum(jnp.dot(tri_ref[2 * d + 1], p, preferred_element_type=F32) for p in h3)
        tot = sum(jnp.dot(sel_ref[...], p, preferred_element_type=F32) for p in h3)
        e_in = jnp.exp(cin)
        e_neg = jnp.exp(-cin)
        e_rem = jnp.exp(rem)
        beta = kk * a
        at_o[...] = (-kk * jnp.exp(cin - lw)).astype(BF16)
        rt_o[...] = (r * e_in).astype(BF16)
        bb_o[...] = (beta * e_neg).astype(BF16)
        kb_o[...] = (kd * e_neg).astype(BF16)
        bh_o[...] = (beta * e_rem).astype(BF16)
        kh_o[...] = (kd * e_rem).astype(BF16)
        pl_o[...] = jnp.exp(tot)
    bonus_o[...] = bonus


def _rwkv_prep(rw, mu, w0, w2big, a0, a2big, g2, k_k, k_a, r_k, bd, tri, sel, seq):
    m, cols = rw.shape
    tm = ROW_TILE
    nblk8 = m // SUBLANES
    w = RWKV_WIDTH
    row = lambda i: (i, 0)
    const2 = lambda i: (0, 0)
    const3 = lambda i: (0, 0, 0)
    full = lambda a: pl.BlockSpec(a.shape, const2 if a.ndim == 2 else const3)
    big_bf = jax.ShapeDtypeStruct((m, w), BF16)
    big_f = jax.ShapeDtypeStruct((m, w), F32)
    pl_shape = jax.ShapeDtypeStruct((m // CHUNK, w), F32)
    per_dir_shapes = [big_bf] * 6 + [pl_shape]
    per_dir_specs = [pl.BlockSpec((tm, w), row)] * 6 + [pl.BlockSpec((tm // CHUNK, w), row)]
    return pl.pallas_call(
        functools.partial(_rwkv_prep_kernel, tiles_per_seq=seq // tm),
        grid=(m // tm,),
        in_specs=[
            pl.BlockSpec((tm, cols), row),
            pl.BlockSpec((SUBLANES, cols), lambda i: (jnp.maximum(i * (tm // SUBLANES) - 1, 0), 0)),
            pl.BlockSpec((SUBLANES, cols), lambda i: (jnp.minimum((i + 1) * (tm // SUBLANES), nblk8 - 1), 0)),
            full(mu), full(w0), full(w2big), full(a0), full(a2big), full(g2),
            full(k_k), full(k_a), full(r_k), full(bd), full(tri), full(sel),
        ],
        out_specs=[pl.BlockSpec((tm, w), row)] * 3 + per_dir_specs * 2,
        out_shape=[big_bf, big_f, big_f] + per_dir_shapes * 2,
        compiler_params=_params(("parallel",)),
        name="rwkv_prep",
    )(rw, rw, rw, mu, w0, w2big, a0, a2big, g2, k_k, k_a, r_k, bd, tri, sel)


def _wkv_kernel(v0_ref, at0, rt0, bb0, kb0, bh0, kh0, pl0,
                v1_ref, at1, rt1, bb1, kb1, bh1, kh1, pl1,
                y0_ref, y1_ref, s_sc):
    c = pl.program_id(1)

    @pl.when(c == 0)
    def _():
        s_sc[...] = jnp.zeros(s_sc.shape, F32)

    n = 2 * CHUNK
    ri = lax.broadcasted_iota(jnp.int32, (n, n), 0)
    ci = lax.broadcasted_iota(jnp.int32, (n, n), 1)
    ti, tj = ri % CHUNK, ci % CHUNK
    same = (ri // CHUNK) == (ci // CHUNK)
    eye = ri == ci
    head0 = lax.broadcasted_iota(jnp.int32, (CHUNK, LANES), 1) < RWKV_HEAD_DIM

    def stack(x):
        zero = jnp.zeros_like(x)
        return jnp.concatenate([jnp.where(head0, x, zero), jnp.where(head0, zero, x)], axis=0)

    dirs = ((v0_ref, at0, rt0, bb0, kb0, bh0, kh0, pl0, y0_ref),
            (v1_ref, at1, rt1, bb1, kb1, bh1, kh1, pl1, y1_ref))
    units = []
    for d, (v_ref, at_r, rt_r, bb_r, kb_r, bh_r, kh_r, pl_r, y_ref) in enumerate(dirs):
        before = (ti > tj) if d == 0 else (ti < tj)
        strict = same & before
        incl = strict | eye
        levels = []
        b = 1
        while b < CHUNK:
            blk = same & ((ti // (2 * b)) == (tj // (2 * b)))
            hi, hj = (ti // b) % 2, (tj // b) % 2
            levels.append(blk & ((hi == 1) & (hj == 0) if d == 0 else (hi == 0) & (hj == 1)))
            b *= 2
        for p in range(RWKV_PAIRS):
            units.append((d, p, slice(p * LANES, (p + 1) * LANES), strict, incl, levels,
                          (v_ref, at_r, rt_r, bb_r, kb_r, bh_r, kh_r, pl_r, y_ref)))
    nu = range(len(units))

    def stacked(k):
        return [stack(u[6][k][:, u[2]]) for u in units]

    v_s, at_s, rt_s, bb_s, kb_s, bh_s, kh_s = (stacked(k) for k in range(7))
    a_ab = [jnp.where(units[i][3], _mm_nt(at_s[i], bb_s[i]), 0.0) for i in nu]
    a_ak = [jnp.where(units[i][3], _mm_nt(at_s[i], kb_s[i]), 0.0).astype(BF16) for i in nu]
    a_rb = [jnp.where(units[i][4], _mm_nt(rt_s[i], bb_s[i]), 0.0).astype(BF16) for i in nu]
    a_rk = [jnp.where(units[i][4], _mm_nt(rt_s[i], kb_s[i]), 0.0).astype(BF16) for i in nu]
    t_inv = [jnp.where(eye, 1.0, jnp.where(units[i][5][0], a_ab[i], 0.0)) for i in nu]
    for lv in range(1, len(units[0][5])):
        t_b = [t.astype(BF16) for t in t_inv]
        half = [_mm(t_b[i], jnp.where(units[i][5][lv], a_ab[i], 0.0)) for i in nu]
        t_inv = [t_inv[i] + _mm(half[i], t_b[i]) for i in nu]
    t_b = [t.astype(BF16) for t in t_inv]
    akv = [_mm(a_ak[i], v_s[i]).astype(BF16) for i in nu]
    wu = [_mm(t_b[i], jnp.concatenate([at_s[i], akv[i]], axis=1)).astype(BF16) for i in nu]
    qy = [_mm(a_rb[i], wu[i]) for i in nu]
    y0 = [qy[i][:, LANES:] + _mm(a_rk[i], v_s[i]) for i in nu]
    qt = [rt_s[i].astype(F32) + qy[i][:, :LANES] for i in nu]
    mn = [_mm_tn(wu[i], bh_s[i]) for i in nu]
    n0t = [mn[i][LANES:, :] + _mm_tn(v_s[i], kh_s[i]) for i in nu]
    for i in nu:
        d, p, sl = units[i][0], units[i][1], units[i][2]
        pl_r, y_ref = units[i][6][7], units[i][6][8]
        st = s_sc[d, p]
        ys = _mm_nt(qt[i], st) + y0[i]
        y_ref[:, sl] = ys[:CHUNK] + ys[CHUNK:]
        s_sc[d, p] = st * pl_r[0, :, sl] + _mm(st, mn[i][:LANES, :]) + n0t[i]


def _wkv(v, dir0, dir1, batch, seq):
    m, w = v.shape
    nc = seq // CHUNK
    fmap = lambda b, c: (b * nc + c, 0)
    bmap = lambda b, c: (b * nc + nc - 1 - c, 0)
    fmap3 = lambda b, c: (b * nc + c, 0, 0)
    bmap3 = lambda b, c: (b * nc + nc - 1 - c, 0, 0)
    blk = lambda im: pl.BlockSpec((CHUNK, w), im)
    in_specs = ([blk(fmap)] * 7 + [pl.BlockSpec((1, 1, w), fmap3)]
                + [blk(bmap)] * 7 + [pl.BlockSpec((1, 1, w), bmap3)])
    return pl.pallas_call(
        _wkv_kernel,
        grid=(batch, nc),
        in_specs=in_specs,
        out_specs=[blk(fmap), blk(bmap)],
        out_shape=[jax.ShapeDtypeStruct((m, w), F32)] * 2,
        scratch_shapes=[pltpu.VMEM((2, RWKV_PAIRS, LANES, LANES), F32)],
        compiler_params=_params(("parallel", "arbitrary")),
        name="wkv_scan",
    )(v, *dir0, v, *dir1)


def _outproj_kernel(x_ref, ya_ref, yf_ref, yb_ref, bonus_ref, g_ref, gnw_ref, gnb_ref, bd_ref,
                    wout_ref, pg_ref, o_ref):
    bd = bd_ref[...]
    inv_n = 1.0 / RWKV_HEAD_DIM
    y = yf_ref[...] + yb_ref[...]
    mean = _seg_sum(y, bd) * inv_n
    yc = y - mean
    var = _seg_sum(yc * yc, bd) * inv_n
    yn = yc * lax.rsqrt(var + GN_EPS) * gnw_ref[...] + gnb_ref[...]
    yb = ((yn + bonus_ref[...]) * g_ref[...]).astype(BF16)
    mix = (jnp.dot(ya_ref[...], wout_ref[:DIFF_WIDTH, :], preferred_element_type=F32)
           + jnp.dot(yb, wout_ref[DIFF_WIDTH:, :], preferred_element_type=F32))
    o_ref[...] = x_ref[...] + _rms(mix, pg_ref[...])


def _out_proj(x2, ya, yf, yb, bonus, g, gn_w, gn_b, bd, wout, post_g):
    m, d = x2.shape
    tm = ROW_TILE
    w = RWKV_WIDTH
    row = lambda i: (i, 0)
    const = lambda i: (0, 0)
    full = lambda a: pl.BlockSpec(a.shape, const)
    return pl.pallas_call(
        _outproj_kernel,
        grid=(m // tm,),
        in_specs=[pl.BlockSpec((tm, d), row), pl.BlockSpec((tm, DIFF_WIDTH), row)]
                 + [pl.BlockSpec((tm, w), row)] * 4
                 + [full(gn_w), full(gn_b), full(bd), full(wout), full(post_g)],
        out_specs=pl.BlockSpec((tm, d), row),
        out_shape=jax.ShapeDtypeStruct((m, d), F32),
        compiler_params=_params(("parallel",)),
        name="out_proj",
    )(x2, ya, yf, yb, bonus, g, gn_w, gn_b, bd, wout, post_g)


def _gelu_tanh(x):
    c = math.sqrt(2.0 / math.pi)
    return 0.5 * x * (1.0 + jnp.tanh(c * (x + 0.044715 * (x * x * x))))


def _ffn_kernel(x_ref, xp_ref, xn_ref, g_ref, wup_ref, cw_ref, cb_ref, wdn_ref, pg_ref, o_ref,
                *, tiles_per_seq, d_ff):
    tm = x_ref.shape[0]
    i = pl.program_id(0)
    first = (i % tiles_per_seq) == 0
    last = (i % tiles_per_seq) == tiles_per_seq - 1
    g = g_ref[...]
    x = x_ref[...]
    hb = _rms(x, g).astype(BF16)
    hp = _rms(xp_ref[...], g).astype(BF16)
    hn = _rms(xn_ref[...], g).astype(BF16)
    hext = jnp.concatenate([hp, hb, hn], axis=0)
    ext = tm + 2 * SUBLANES
    row = lax.broadcasted_iota(jnp.int32, (ext, 1), 0)
    kill_prev = jnp.logical_and(first, row == SUBLANES)
    kill_next = jnp.logical_and(last, row == SUBLANES + tm - 1)
    acc = jnp.zeros((tm, o_ref.shape[1]), F32)
    for j in range(d_ff // FF_CHUNK):
        cs = slice(j * FF_CHUNK, (j + 1) * FF_CHUNK)
        gate = jnp.dot(hext, wup_ref[:, cs], preferred_element_type=F32)
        up = jnp.dot(hb, wup_ref[:, d_ff + j * FF_CHUNK:d_ff + (j + 1) * FF_CHUNK],
                     preferred_element_type=F32)
        gp = jnp.where(kill_prev, 0.0, pltpu.roll(gate, 1, 0))
        gn = jnp.where(kill_next, 0.0, pltpu.roll(gate, ext - 1, 0))
        conv = cw_ref[0:1, cs] * gp + cw_ref[1:2, cs] * gate + cw_ref[2:3, cs] * gn + cb_ref[:, cs]
        conv = conv[SUBLANES:SUBLANES + tm, :]
        act = (_gelu_tanh(conv) * up).astype(BF16)
        acc = acc + jnp.dot(act, wdn_ref[cs, :], preferred_element_type=F32)
    o_ref[...] = x + _rms(acc, pg_ref[...])


def _ffn(x1, pre_g, wup, conv_w, conv_b, wdn, post_g, seq):
    m, d = x1.shape
    tm = ROW_TILE
    d_ff = wdn.shape[0]
    nblk8 = m // SUBLANES
    row = lambda i: (i, 0)
    const = lambda i: (0, 0)
    full = lambda a: pl.BlockSpec(a.shape, const)
    return pl.pallas_call(
        functools.partial(_ffn_kernel, tiles_per_seq=seq // tm, d_ff=d_ff),
        grid=(m // tm,),
        in_specs=[
            pl.BlockSpec((tm, d), row),
            pl.BlockSpec((SUBLANES, d), lambda i: (jnp.maximum(i * (tm // SUBLANES) - 1, 0), 0)),
            pl.BlockSpec((SUBLANES, d), lambda i: (jnp.minimum((i + 1) * (tm // SUBLANES), nblk8 - 1), 0)),
            full(pre_g), full(wup), full(conv_w), full(conv_b), full(wdn), full(post_g),
        ],
        out_specs=pl.BlockSpec((tm, d), row),
        out_shape=jax.ShapeDtypeStruct((m, d), F32),
        compiler_params=_params(("parallel",)),
        name="conv_ffn",
    )(x1, x1, x1, pre_g, wup, conv_w, conv_b, wdn, post_g)


def _rope_tables(seq):
    half = DIFF_QK_DIM // 2
    pos = jnp.arange(seq, dtype=F32)
    inv = ROPE_THETA ** (-jnp.arange(half, dtype=F32) / half)
    ang = pos[:, None] * inv[None, :]
    cos, sin = jnp.cos(ang), jnp.sin(ang)
    reps = LANES // DIFF_QK_DIM
    cos_t = jnp.tile(jnp.concatenate([cos, cos], axis=-1), (1, reps))
    sin_t = jnp.tile(jnp.concatenate([-sin, sin], axis=-1), (1, reps))
    return cos_t, sin_t


def _scan_constants(tm):
    t = jnp.arange(tm)
    same = (t[:, None] // CHUNK) == (t[None, :] // CHUNK)
    lower = t[:, None] >= t[None, :]
    upper = t[:, None] <= t[None, :]
    tri = jnp.stack([same & lower, same & ~lower,
                     same & upper, same & ~upper])
    sel = (jnp.arange(tm // CHUNK)[:, None] == (t[None, :] // CHUNK))
    ch = jnp.arange(RWKV_WIDTH)
    bd = (ch[:, None] // RWKV_HEAD_DIM) == (ch[None, :] // RWKV_HEAD_DIM)
    return tri.astype(BF16), sel.astype(BF16), bd.astype(BF16)


def _lora_blockdiag(w2):
    z = jnp.zeros_like(w2[0])
    return jnp.concatenate([jnp.concatenate([w2[0], z], axis=1),
                            jnp.concatenate([z, w2[1]], axis=1)], axis=0).astype(BF16)


def kernel(x, pre_mix_norm, post_mix_norm, pre_ffn_norm, post_ffn_norm, w_in, diff_lambda_q1, diff_lambda_k1, diff_lambda_q2, diff_lambda_k2, diff_subln, rwkv_mu, rwkv_w0, rwkv_w2, rwkv_a0, rwkv_a2, rwkv_g2, rwkv_k_k, rwkv_k_a, rwkv_r_k, rwkv_gn_w, rwkv_gn_b, w_out, w_up, ffn_conv_w, ffn_conv_b, w_down):
    batch, seq, d = x.shape
    depth = w_in.shape[0]
    assert seq % ROW_TILE == 0 and seq % ATTN_TK == 0 and ROW_TILE % CHUNK == 0
    m = batch * seq
    nc_total = m // CHUNK
    cos_t, sin_t = _rope_tables(seq)
    tri, sel, bd = _scan_constants(ROW_TILE)
    x2 = x.reshape(m, d)
    diff_cols = 3 * DIFF_WIDTH
    row2 = lambda a: a.reshape(1, -1)
    for l in range(depth):
        lambda_init = 0.8 - 0.6 * math.exp(-0.3 * l)
        wqkv = w_in[l][:, :diff_cols].astype(BF16)
        wrw = w_in[l][:, diff_cols:].astype(BF16)
        q0, q1, k, v, rw = _in_proj(x2, row2(pre_mix_norm[l]), wqkv, wrw, cos_t, sin_t, seq)
        lam_p = jnp.stack([diff_lambda_q1[l], diff_lambda_k1[l], diff_lambda_q2[l], diff_lambda_k2[l]])
        ya = _attention(lam_p, diff_subln[l].reshape(-1, 1), q0, q1, k, _values_transposed(v, batch, seq),
                        batch, seq, lambda_init)
        prep = _rwkv_prep(rw, row2(rwkv_mu[l]), rwkv_w0[l], _lora_blockdiag(rwkv_w2[l]), rwkv_a0[l],
                          _lora_blockdiag(rwkv_a2[l]), rwkv_g2[l].astype(BF16), row2(rwkv_k_k[l]),
                          row2(rwkv_k_a[l]), row2(rwkv_r_k[l]), bd, tri, sel, seq)
        vb, g, bonus = prep[0], prep[1], prep[2]
        dir0 = list(prep[3:9]) + [prep[9].reshape(nc_total, 1, RWKV_WIDTH)]
        dir1 = list(prep[10:16]) + [prep[16].reshape(nc_total, 1, RWKV_WIDTH)]
        yf, yb = _wkv(vb, dir0, dir1, batch, seq)
        x2 = _out_proj(x2, ya, yf, yb, bonus, g, row2(rwkv_gn_w[l]), row2(rwkv_gn_b[l]), bd,
                       w_out[l].astype(BF16), row2(post_mix_norm[l]))
        x2 = _ffn(x2, row2(pre_ffn_norm[l]), w_up[l].astype(BF16), ffn_conv_w[l], row2(ffn_conv_b[l]),
                  w_down[l].astype(BF16), row2(post_ffn_norm[l]), seq)
    return x2.reshape(batch, seq, d)
```

```python
import functools
import math

import jax
import jax.numpy as jnp
from jax import lax
from jax.experimental import pallas as pl
from jax.experimental.pallas import tpu as pltpu

F32 = jnp.float32
BF16 = jnp.bfloat16

LANES = 128
SUBLANES = 8
VMEM_LIMIT = 56 * 1024 * 1024

DIFF_QK_DIM = 64
DIFF_V_DIM = 128
DIFF_HEADS = 4
DIFF_WIDTH = DIFF_HEADS * DIFF_V_DIM
RWKV_HEAD_DIM = 64
RWKV_WIDTH = 512
RWKV_PAIRS = RWKV_WIDTH // LANES
DECAY_LORA = 64
ICLR_LORA = 64
GATE_LORA = 128
ROPE_THETA = 10000.0
NORM_EPS = 1e-6
GN_EPS = 64e-5
CHUNK = 64

ROW_TILE = 512
FFN_ROW_TILE = 1024
ATTN_TQ = 256
ATTN_TK = 1024
ATTN_VROWS = 144
FF_CHUNK = 256


def _rms(x, g):
    return x * lax.rsqrt(jnp.mean(x * x, axis=-1, keepdims=True) + NORM_EPS) * g


def _mm(a, b):
    return jnp.dot(a.astype(BF16), b.astype(BF16), preferred_element_type=F32)


def _mm_nt(a, b):
    return lax.dot_general(a.astype(BF16), b.astype(BF16), (((1,), (1,)), ((), ())),
                           preferred_element_type=F32)


def _mm_tn(a, b):
    return lax.dot_general(a.astype(BF16), b.astype(BF16), (((0,), (0,)), ((), ())),
                           preferred_element_type=F32)


def _split2(x):
    hi = x.astype(BF16)
    lo = (x - hi.astype(F32)).astype(BF16)
    return hi, lo


def _split3(x):
    hi = x.astype(BF16)
    r1 = x - hi.astype(F32)
    mid = r1.astype(BF16)
    lo = (r1 - mid.astype(F32)).astype(BF16)
    return hi, mid, lo


def _seg_sum(x, bd):
    hi, lo = _split2(x)
    return (jnp.dot(hi, bd, preferred_element_type=F32)
            + jnp.dot(lo, bd, preferred_element_type=F32))


def _params(semantics):
    return pltpu.CompilerParams(dimension_semantics=semantics, vmem_limit_bytes=VMEM_LIMIT)


def _inproj_kernel(x_ref, g_ref, wqkv_ref, wrw_ref, cos_ref, sin_ref,
                   q0_ref, q1_ref, k_ref, v_ref, rw_ref):
    tm = x_ref.shape[0]
    hb = _rms(x_ref[...], g_ref[...]).astype(BF16)
    qk = jnp.dot(hb, wqkv_ref[:, :2 * DIFF_WIDTH], preferred_element_type=F32)
    cos = cos_ref[...]
    sin = sin_ref[...]
    lane = lax.broadcasted_iota(jnp.int32, (tm, LANES), 1)
    first_half = (lane % DIFF_QK_DIM) < (DIFF_QK_DIM // 2)
    comp0 = lane < DIFF_QK_DIM
    scale = DIFF_QK_DIM ** -0.5
    for j in range(2 * DIFF_HEADS):
        t = qk[:, j * LANES:(j + 1) * LANES]
        partner = jnp.where(first_half, pltpu.roll(t, LANES - DIFF_QK_DIM // 2, 1),
                            pltpu.roll(t, DIFF_QK_DIM // 2, 1))
        o = t * cos + partner * sin
        if j < DIFF_HEADS:
            o = o * scale
            q0_ref[:, j * LANES:(j + 1) * LANES] = jnp.where(comp0, o, 0.0).astype(BF16)
            q1_ref[:, j * LANES:(j + 1) * LANES] = jnp.where(comp0, 0.0, o).astype(BF16)
        else:
            jj = j - DIFF_HEADS
            k_ref[:, jj * LANES:(jj + 1) * LANES] = o.astype(BF16)
    v_ref[...] = jnp.dot(hb, wqkv_ref[:, 2 * DIFF_WIDTH:], preferred_element_type=F32).astype(BF16)
    rw_ref[...] = jnp.dot(hb, wrw_ref[...], preferred_element_type=F32)


def _in_proj(x2, g, wqkv, wrw, cos_t, sin_t, seq):
    m, d = x2.shape
    tm = ROW_TILE
    tps = seq // tm
    rw_cols = wrw.shape[1]
    row = lambda i: (i, 0)
    const = lambda i: (0, 0)
    return pl.pallas_call(
        _inproj_kernel,
        grid=(m // tm,),
        in_specs=[
            pl.BlockSpec((tm, d), row),
            pl.BlockSpec((1, d), const, pipeline_mode=pl.Buffered(1)),
            pl.BlockSpec(wqkv.shape, const, pipeline_mode=pl.Buffered(1)),
            pl.BlockSpec(wrw.shape, const, pipeline_mode=pl.Buffered(1)),
            pl.BlockSpec((tm, LANES), lambda i: (i % tps, 0)),
            pl.BlockSpec((tm, LANES), lambda i: (i % tps, 0)),
        ],
        out_specs=[
            pl.BlockSpec((tm, DIFF_WIDTH), row),
            pl.BlockSpec((tm, DIFF_WIDTH), row),
            pl.BlockSpec((tm, DIFF_WIDTH), row),
            pl.BlockSpec((tm, DIFF_WIDTH), row),
            pl.BlockSpec((tm, rw_cols), row),
        ],
        out_shape=[
            jax.ShapeDtypeStruct((m, DIFF_WIDTH), BF16),
            jax.ShapeDtypeStruct((m, DIFF_WIDTH), BF16),
            jax.ShapeDtypeStruct((m, DIFF_WIDTH), BF16),
            jax.ShapeDtypeStruct((m, DIFF_WIDTH), BF16),
            jax.ShapeDtypeStruct((m, rw_cols), F32),
        ],
        compiler_params=_params(("parallel",)),
        name="in_proj",
    )(x2, g, wqkv, wrw, cos_t, sin_t)


def _attn_kernel(lam_ref, sub_ref, q0_ref, q1_ref, q0n_ref, q1n_ref, k_ref, vt_ref, o_ref,
                 s_sc, mx_sc, m_sc, acc_sc, *, tk, lambda_init):
    seq = k_ref.shape[0]
    nk = seq // tk
    qs = (q0_ref[...], q1_ref[...])

    def scores(qpair, i, slot):
        start = i * tk
        ks = k_ref[pl.ds(start, tk), :]
        maxima = []
        for c in range(2):
            st = lax.dot_general(ks, qpair[c], (((1,), (1,)), ((), ())), preferred_element_type=F32)
            s_sc[slot, c] = st
            maxima.append(jnp.max(st, axis=0, keepdims=True))
        return tuple(maxima)

    @pl.when(pl.program_id(2) == 0)
    def _():
        first = scores(qs, 0, 0)
        mx_sc[0] = first[0]
        mx_sc[1] = first[1]

    m_sc[...] = jnp.full(m_sc.shape, -jnp.inf, F32)
    acc_sc[...] = jnp.zeros(acc_sc.shape, F32)

    def accumulate(i, slot, maxima):
        start = i * tk
        vt = vt_ref[:, pl.ds(start, tk)]
        for c in range(2):
            m_old = m_sc[c]
            m_new = jnp.maximum(m_old, maxima[c])
            alpha = jnp.exp(m_old - m_new)
            p = jnp.exp(s_sc[slot, c] - m_new).astype(BF16)
            acc_sc[c] = alpha * acc_sc[c] + jnp.dot(vt, p, preferred_element_type=F32)
            m_sc[c] = m_new

    maxima = (mx_sc[0], mx_sc[1])
    for blk in range(nk):
        if blk + 1 < nk:
            ahead = scores(qs, blk + 1, (blk + 1) % 2)
        else:
            ahead = scores((q0n_ref[...], q1n_ref[...]), 0, 0)
        accumulate(blk, blk % 2, maxima)
        maxima = ahead
    mx_sc[0] = maxima[0]
    mx_sc[1] = maxima[1]

    lp = lam_ref[...]
    lam = (jnp.exp(jnp.sum(lp[0:1] * lp[1:2], axis=-1, keepdims=True))
           - jnp.exp(jnp.sum(lp[2:3] * lp[3:4], axis=-1, keepdims=True)) + lambda_init)
    a0 = acc_sc[0]
    a1 = acc_sc[1]
    dv = DIFF_V_DIM
    o = a0[:dv] / a0[dv:dv + 1] - lam * (a1[:dv] / a1[dv:dv + 1])
    o = o * lax.rsqrt(jnp.mean(o * o, axis=0, keepdims=True) + NORM_EPS) * sub_ref[...] * (1.0 - lambda_init)
    o_ref[...] = o.T.astype(BF16)


def _attention(lam_p, subln_col, q0, q1, k, vt, batch, seq, lambda_init):
    tq, tk = ATTN_TQ, min(ATTN_TK, seq // 2)
    assert (seq // tk) % 2 == 0
    nq = seq // tq
    qmap = lambda b, h, i: (b * nq + i, h)
    qnext = lambda b, h, i: (b * nq + jnp.minimum(i + 1, nq - 1), h)
    const = lambda b, h, i: (0, 0)
    return pl.pallas_call(
        functools.partial(_attn_kernel, tk=tk, lambda_init=lambda_init),
        grid=(batch, DIFF_HEADS, nq),
        in_specs=[
            pl.BlockSpec(lam_p.shape, const),
            pl.BlockSpec(subln_col.shape, const),
            pl.BlockSpec((tq, LANES), qmap),
            pl.BlockSpec((tq, LANES), qmap),
            pl.BlockSpec((tq, LANES), qnext),
            pl.BlockSpec((tq, LANES), qnext),
            pl.BlockSpec((seq, LANES), lambda b, h, i: (b, h)),
            pl.BlockSpec((ATTN_VROWS, seq), lambda b, h, i: (b * DIFF_HEADS + h, 0)),
        ],
        out_specs=pl.BlockSpec((tq, LANES), qmap),
        out_shape=jax.ShapeDtypeStruct((batch * seq, DIFF_WIDTH), BF16),
        scratch_shapes=[
            pltpu.VMEM((2, 2, tk, tq), F32),
            pltpu.VMEM((2, 1, tq), F32),
            pltpu.VMEM((2, 1, tq), F32),
            pltpu.VMEM((2, ATTN_VROWS, tq), F32),
        ],
        compiler_params=_params(("parallel", "parallel", "arbitrary")),
        name="diff_attn",
    )(lam_p, subln_col, q0, q1, q0, q1, k, vt)


def _values_transposed(v, batch, seq):
    vt = v.reshape(batch, seq, DIFF_HEADS, DIFF_V_DIM).transpose(0, 2, 3, 1)
    pad = jnp.zeros((batch, DIFF_HEADS, ATTN_VROWS - DIFF_V_DIM, seq), v.dtype).at[:, :, 0, :].set(1)
    return jnp.concatenate([vt, pad], axis=2).reshape(batch * DIFF_HEADS * ATTN_VROWS, seq)


def _sigmoid(x):
    return 1.0 / (1.0 + jnp.exp(-x))


def _rwkv_prep_kernel(z_ref, zp_ref, zn_ref, mu_ref, w0_ref, w2_ref, a0_ref, a2_ref, g2_ref,
                      kk_ref, ka_ref, rk_ref, bd_ref, tri_ref, sel_ref,
                      v_o, g_o, bonus_o,
                      at0_o, rt0_o, bb0_o, kb0_o, bh0_o, kh0_o, pl0_o,
                      at1_o, rt1_o, bb1_o, kb1_o, bh1_o, kh1_o, pl1_o, *, tiles_per_seq):
    tm = z_ref.shape[0]
    w = RWKV_WIDTH
    i = pl.program_id(0)
    first = (i % tiles_per_seq) == 0
    last = (i % tiles_per_seq) == tiles_per_seq - 1
    z = z_ref[...]
    prow = jnp.where(first, 0.0, zp_ref[SUBLANES - 1:SUBLANES, :])
    nrow = jnp.where(last, 0.0, zn_ref[0:1, :])
    row = lax.broadcasted_iota(jnp.int32, (tm, 1), 0)
    zp = jnp.where(row == 0, prow, pltpu.roll(z, 1, 0))
    zn = jnp.where(row == tm - 1, nrow, pltpu.roll(z, tm - 1, 0))
    zs = z + (0.5 * (zp + zn) - z) * mu_ref[...]

    r = zs[:, 0:w]
    k = zs[:, w:2 * w]
    v = zs[:, 2 * w:3 * w]
    o3 = 3 * w
    wd = zs[:, o3:o3 + 2 * DECAY_LORA]
    ad = zs[:, o3 + 2 * DECAY_LORA:o3 + 2 * DECAY_LORA + 2 * ICLR_LORA]
    gd = zs[:, o3 + 2 * DECAY_LORA + 2 * ICLR_LORA:]
    bd = bd_ref[...]

    g_o[...] = _mm(_sigmoid(gd), g2_ref[...])
    v_o[...] = v.astype(BF16)
    kkf = k * kk_ref[...]
    kk = kkf * lax.rsqrt(jnp.maximum(_seg_sum(kkf * kkf, bd), 1e-12))
    wl_all = _mm(jnp.tanh(wd), w2_ref[...])
    al_all = _mm(ad, a2_ref[...])
    outs = ((at0_o, rt0_o, bb0_o, kb0_o, bh0_o, kh0_o, pl0_o),
            (at1_o, rt1_o, bb1_o, kb1_o, bh1_o, kh1_o, pl1_o))
    bonus = jnp.zeros((tm, w), F32)
    for d in range(2):
        at_o, rt_o, bb_o, kb_o, bh_o, kh_o, pl_o = outs[d]
        x = -(w0_ref[d:d + 1, :] + wl_all[:, d * w:(d + 1) * w])
        softplus = jnp.maximum(x, 0.0) + jnp.log(1.0 + jnp.exp(-jnp.abs(x)))
        lw = -jnp.exp(-softplus - 0.5)
        a = _sigmoid(a0_ref[d:d + 1, :] + al_all[:, d * w:(d + 1) * w])
        kd = k * (1.0 + (a - 1.0) * ka_ref[...])
        bonus = bonus + _seg_sum(r * kd * rk_ref[...], bd) * v
        h3 = _split3(lw)
        cin = sum(jnp.dot(tri_ref[2 * d], p, preferred_element_type=F32) for p in h3)
        rem = sum(jnp.dot(tri_ref[2 * d + 1], p, preferred_element_type=F32) for p in h3)
        tot = sum(jnp.dot(sel_ref[...], p, preferred_element_type=F32) for p in h3)
        e_in = jnp.exp(cin)
        e_neg = jnp.exp(-cin)
        e_rem = jnp.exp(rem)
        beta = kk * a
        at_o[...] = (-kk * jnp.exp(cin - lw)).astype(BF16)
        rt_o[...] = (r * e_in).astype(BF16)
        bb_o[...] = (beta * e_neg).astype(BF16)
        kb_o[...] = (kd * e_neg).astype(BF16)
        bh_o[...] = (beta * e_rem).astype(BF16)
        kh_o[...] = (kd * e_rem).astype(BF16)
        pl_o[...] = jnp.exp(tot)
    bonus_o[...] = bonus


def _rwkv_prep(rw, mu, w0, w2big, a0, a2big, g2, k_k, k_a, r_k, bd, tri, sel, seq):
    m, cols = rw.shape
    tm = ROW_TILE
    nblk8 = m // SUBLANES
    w = RWKV_WIDTH
    row = lambda i: (i, 0)
    const2 = lambda i: (0, 0)
    const3 = lambda i: (0, 0, 0)
    full = lambda a: pl.BlockSpec(a.shape, const2 if a.ndim == 2 else const3, pipeline_mode=pl.Buffered(1))
    big_bf = jax.ShapeDtypeStruct((m, w), BF16)
    big_f = jax.ShapeDtypeStruct((m, w), F32)
    pl_shape = jax.ShapeDtypeStruct((m // CHUNK, w), F32)
    per_dir_shapes = [big_bf] * 6 + [pl_shape]
    per_dir_specs = [pl.BlockSpec((tm, w), row)] * 6 + [pl.BlockSpec((tm // CHUNK, w), row)]
    return pl.pallas_call(
        functools.partial(_rwkv_prep_kernel, tiles_per_seq=seq // tm),
        grid=(m // tm,),
        in_specs=[
            pl.BlockSpec((tm, cols), row),
            pl.BlockSpec((SUBLANES, cols), lambda i: (jnp.maximum(i * (tm // SUBLANES) - 1, 0), 0)),
            pl.BlockSpec((SUBLANES, cols), lambda i: (jnp.minimum((i + 1) * (tm // SUBLANES), nblk8 - 1), 0)),
            full(mu), full(w0), full(w2big), full(a0), full(a2big), full(g2),
            full(k_k), full(k_a), full(r_k), full(bd), full(tri), full(sel),
        ],
        out_specs=[pl.BlockSpec((tm, w), row)] * 3 + per_dir_specs * 2,
        out_shape=[big_bf, big_f, big_f] + per_dir_shapes * 2,
        compiler_params=_params(("parallel",)),
        name="rwkv_prep",
    )(rw, rw, rw, mu, w0, w2big, a0, a2big, g2, k_k, k_a, r_k, bd, tri, sel)


def _wkv_kernel(v0_ref, at0, rt0, bb0, kb0, bh0, kh0, pl0,
                v1_ref, at1, rt1, bb1, kb1, bh1, kh1, pl1,
                y0_ref, y1_ref, s_sc):
    c = pl.program_id(1)

    @pl.when(c == 0)
    def _():
        s_sc[...] = jnp.zeros(s_sc.shape, F32)

    n = 2 * CHUNK
    ri = lax.broadcasted_iota(jnp.int32, (n, n), 0)
    ci = lax.broadcasted_iota(jnp.int32, (n, n), 1)
    ti, tj = ri % CHUNK, ci % CHUNK
    same = (ri // CHUNK) == (ci // CHUNK)
    eye = ri == ci
    head0 = lax.broadcasted_iota(jnp.int32, (CHUNK, LANES), 1) < RWKV_HEAD_DIM

    def stack(x):
        zero = jnp.zeros_like(x)
        return jnp.concatenate([jnp.where(head0, x, zero), jnp.where(head0, zero, x)], axis=0)

    dirs = ((v0_ref, at0, rt0, bb0, kb0, bh0, kh0, pl0, y0_ref),
            (v1_ref, at1, rt1, bb1, kb1, bh1, kh1, pl1, y1_ref))
    units = []
    for d, (v_ref, at_r, rt_r, bb_r, kb_r, bh_r, kh_r, pl_r, y_ref) in enumerate(dirs):
        before = (ti > tj) if d == 0 else (ti < tj)
        strict = same & before
        incl = strict | eye
        levels = []
        b = 1
        while b < CHUNK:
            blk = same & ((ti // (2 * b)) == (tj // (2 * b)))
            hi, hj = (ti // b) % 2, (tj // b) % 2
            levels.append(blk & ((hi == 1) & (hj == 0) if d == 0 else (hi == 0) & (hj == 1)))
            b *= 2
        for p in range(RWKV_PAIRS):
            units.append((d, p, slice(p * LANES, (p + 1) * LANES), strict, incl, levels,
                          (v_ref, at_r, rt_r, bb_r, kb_r, bh_r, kh_r, pl_r, y_ref)))
    nu = range(len(units))

    def stacked(k):
        return [stack(u[6][k][:, u[2]]) for u in units]

    v_s, at_s, rt_s, bb_s, kb_s, bh_s, kh_s = (stacked(k) for k in range(7))
    a_ab = [jnp.where(units[i][3], _mm_nt(at_s[i], bb_s[i]), 0.0) for i in nu]
    a_ak = [jnp.where(units[i][3], _mm_nt(at_s[i], kb_s[i]), 0.0).astype(BF16) for i in nu]
    a_rb = [jnp.where(units[i][4], _mm_nt(rt_s[i], bb_s[i]), 0.0).astype(BF16) for i in nu]
    a_rk = [jnp.where(units[i][4], _mm_nt(rt_s[i], kb_s[i]), 0.0).astype(BF16) for i in nu]
    t_inv = [jnp.where(eye, 1.0, jnp.where(units[i][5][0], a_ab[i], 0.0)) for i in nu]
    for lv in range(1, len(units[0][5])):
        t_b = [t.astype(BF16) for t in t_inv]
        half = [_mm(t_b[i], jnp.where(units[i][5][lv], a_ab[i], 0.0)) for i in nu]
        t_inv = [t_inv[i] + _mm(half[i], t_b[i]) for i in nu]
    t_b = [t.astype(BF16) for t in t_inv]
    akv = [_mm(a_ak[i], v_s[i]).astype(BF16) for i in nu]
    wu = [_mm(t_b[i], jnp.concatenate([at_s[i], akv[i]], axis=1)).astype(BF16) for i in nu]
    qy = [_mm(a_rb[i], wu[i]) for i in nu]
    y0 = [qy[i][:, LANES:] + _mm(a_rk[i], v_s[i]) for i in nu]
    qt = [rt_s[i].astype(F32) + qy[i][:, :LANES] for i in nu]
    mn = [_mm_tn(wu[i], bh_s[i]) for i in nu]
    n0t = [mn[i][LANES:, :] + _mm_tn(v_s[i], kh_s[i]) for i in nu]
    for i in nu:
        d, p, sl = units[i][0], units[i][1], units[i][2]
        pl_r, y_ref = units[i][6][7], units[i][6][8]
        st = s_sc[d, p]
        ys = _mm_nt(qt[i], st) + y0[i]
        y_ref[:, sl] = ys[:CHUNK] + ys[CHUNK:]
        s_sc[d, p] = st * pl_r[0, :, sl] + _mm(st, mn[i][:LANES, :]) + n0t[i]


def _wkv(v, dir0, dir1, batch, seq):
    m, w = v.shape
    nc = seq // CHUNK
    fmap = lambda b, c: (b * nc + c, 0)
    bmap = lambda b, c: (b * nc + nc - 1 - c, 0)
    fmap3 = lambda b, c: (b * nc + c, 0, 0)
    bmap3 = lambda b, c: (b * nc + nc - 1 - c, 0, 0)
    blk = lambda im: pl.BlockSpec((CHUNK, w), im)
    in_specs = ([blk(fmap)] * 7 + [pl.BlockSpec((1, 1, w), fmap3)]
                + [blk(bmap)] * 7 + [pl.BlockSpec((1, 1, w), bmap3)])
    return pl.pallas_call(
        _wkv_kernel,
        grid=(batch, nc),
        in_specs=in_specs,
        out_specs=[blk(fmap), blk(bmap)],
        out_shape=[jax.ShapeDtypeStruct((m, w), F32)] * 2,
        scratch_shapes=[pltpu.VMEM((2, RWKV_PAIRS, LANES, LANES), F32)],
        compiler_params=_params(("parallel", "arbitrary")),
        name="wkv_scan",
    )(v, *dir0, v, *dir1)


def _outproj_kernel(x_ref, ya_ref, yf_ref, yb_ref, bonus_ref, g_ref, gnw_ref, gnb_ref, bd_ref,
                    wout_ref, pg_ref, o_ref):
    bd = bd_ref[...]
    inv_n = 1.0 / RWKV_HEAD_DIM
    y = yf_ref[...] + yb_ref[...]
    mean = _seg_sum(y, bd) * inv_n
    yc = y - mean
    var = _seg_sum(yc * yc, bd) * inv_n
    yn = yc * lax.rsqrt(var + GN_EPS) * gnw_ref[...] + gnb_ref[...]
    yb = ((yn + bonus_ref[...]) * g_ref[...]).astype(BF16)
    mix = (jnp.dot(ya_ref[...], wout_ref[:DIFF_WIDTH, :], preferred_element_type=F32)
           + jnp.dot(yb, wout_ref[DIFF_WIDTH:, :], preferred_element_type=F32))
    o_ref[...] = x_ref[...] + _rms(mix, pg_ref[...])


def _out_proj(x2, ya, yf, yb, bonus, g, gn_w, gn_b, bd, wout, post_g):
    m, d = x2.shape
    tm = ROW_TILE
    w = RWKV_WIDTH
    row = lambda i: (i, 0)
    const = lambda i: (0, 0)
    full = lambda a: pl.BlockSpec(a.shape, const, pipeline_mode=pl.Buffered(1))
    return pl.pallas_call(
        _outproj_kernel,
        grid=(m // tm,),
        in_specs=[pl.BlockSpec((tm, d), row), pl.BlockSpec((tm, DIFF_WIDTH), row)]
                 + [pl.BlockSpec((tm, w), row)] * 4
                 + [full(gn_w), full(gn_b), full(bd), full(wout), full(post_g)],
        out_specs=pl.BlockSpec((tm, d), row),
        out_shape=jax.ShapeDtypeStruct((m, d), F32),
        compiler_params=_params(("parallel",)),
        name="out_proj",
    )(x2, ya, yf, yb, bonus, g, gn_w, gn_b, bd, wout, post_g)


def _gelu_tanh(x):
    c = math.sqrt(2.0 / math.pi)
    return 0.5 * x * (1.0 + jnp.tanh(c * (x + 0.044715 * (x * x * x))))


def _ffn_kernel(x_ref, xp_ref, xn_ref, g_ref, wup_ref, cw_ref, cb_ref, wdn_ref, pg_ref, o_ref,
                *, tiles_per_seq, d_ff):
    tm = x_ref.shape[0]
    i = pl.program_id(0)
    first = (i % tiles_per_seq) == 0
    last = (i % tiles_per_seq) == tiles_per_seq - 1
    g = g_ref[...]
    x = x_ref[...]
    hb = _rms(x, g).astype(BF16)
    hp = _rms(xp_ref[...], g).astype(BF16)
    hn = _rms(xn_ref[...], g).astype(BF16)
    hext = jnp.concatenate([hp, hb, hn], axis=0)
    ext = tm + 2 * SUBLANES
    row = lax.broadcasted_iota(jnp.int32, (ext, 1), 0)
    kill_prev = jnp.logical_and(first, row == SUBLANES)
    kill_next = jnp.logical_and(last, row == SUBLANES + tm - 1)
    acc = jnp.zeros((tm, o_ref.shape[1]), F32)
    for j in range(d_ff // FF_CHUNK):
        cs = slice(j * FF_CHUNK, (j + 1) * FF_CHUNK)
        gate = jnp.dot(hext, wup_ref[:, cs], preferred_element_type=F32)
        up = jnp.dot(hb, wup_ref[:, d_ff + j * FF_CHUNK:d_ff + (j + 1) * FF_CHUNK],
                     preferred_element_type=F32)
        gp = jnp.where(kill_prev, 0.0, pltpu.roll(gate, 1, 0))
        gn = jnp.where(kill_next, 0.0, pltpu.roll(gate, ext - 1, 0))
        conv = cw_ref[0:1, cs] * gp + cw_ref[1:2, cs] * gate + cw_ref[2:3, cs] * gn + cb_ref[:, cs]
        conv = conv[SUBLANES:SUBLANES + tm, :]
        act = (_gelu_tanh(conv) * up).astype(BF16)
        acc = acc + jnp.dot(act, wdn_ref[cs, :], preferred_element_type=F32)
    o_ref[...] = x + _rms(acc, pg_ref[...])


def _ffn(x1, pre_g, wup, conv_w, conv_b, wdn, post_g, seq):
    m, d = x1.shape
    tm = min(FFN_ROW_TILE, seq)
    d_ff = wdn.shape[0]
    nblk8 = m // SUBLANES
    row = lambda i: (i, 0)
    const = lambda i: (0, 0)
    full = lambda a: pl.BlockSpec(a.shape, const, pipeline_mode=pl.Buffered(1))
    return pl.pallas_call(
        functools.partial(_ffn_kernel, tiles_per_seq=seq // tm, d_ff=d_ff),
        grid=(m // tm,),
        in_specs=[
            pl.BlockSpec((tm, d), row),
            pl.BlockSpec((SUBLANES, d), lambda i: (jnp.maximum(i * (tm // SUBLANES) - 1, 0), 0)),
            pl.BlockSpec((SUBLANES, d), lambda i: (jnp.minimum((i + 1) * (tm // SUBLANES), nblk8 - 1), 0)),
            full(pre_g), full(wup), full(conv_w), full(conv_b), full(wdn), full(post_g),
        ],
        out_specs=pl.BlockSpec((tm, d), row),
        out_shape=jax.ShapeDtypeStruct((m, d), F32),
        compiler_params=_params(("parallel",)),
        name="conv_ffn",
    )(x1, x1, x1, pre_g, wup, conv_w, conv_b, wdn, post_g)


def _rope_tables(seq):
    half = DIFF_QK_DIM // 2
    pos = jnp.arange(seq, dtype=F32)
    inv = ROPE_THETA ** (-jnp.arange(half, dtype=F32) / half)
    ang = pos[:, None] * inv[None, :]
    cos, sin = jnp.cos(ang), jnp.sin(ang)
    reps = LANES // DIFF_QK_DIM
    cos_t = jnp.tile(jnp.concatenate([cos, cos], axis=-1), (1, reps))
    sin_t = jnp.tile(jnp.concatenate([-sin, sin], axis=-1), (1, reps))
    return cos_t, sin_t


def _scan_constants(tm):
    t = jnp.arange(tm)
    same = (t[:, None] // CHUNK) == (t[None, :] // CHUNK)
    lower = t[:, None] >= t[None, :]
    upper = t[:, None] <= t[None, :]
    tri = jnp.stack([same & lower, same & ~lower,
                     same & upper, same & ~upper])
    sel = (jnp.arange(tm // CHUNK)[:, None] == (t[None, :] // CHUNK))
    ch = jnp.arange(RWKV_WIDTH)
    bd = (ch[:, None] // RWKV_HEAD_DIM) == (ch[None, :] // RWKV_HEAD_DIM)
    return tri.astype(BF16), sel.astype(BF16), bd.astype(BF16)


def _lora_blockdiag(w2):
    z = jnp.zeros_like(w2[0])
    return jnp.concatenate([jnp.concatenate([w2[0], z], axis=1),
                            jnp.concatenate([z, w2[1]], axis=1)], axis=0).astype(BF16)


def kernel(x, pre_mix_norm, post_mix_norm, pre_ffn_norm, post_ffn_norm, w_in, diff_lambda_q1, diff_lambda_k1, diff_lambda_q2, diff_lambda_k2, diff_subln, rwkv_mu, rwkv_w0, rwkv_w2, rwkv_a0, rwkv_a2, rwkv_g2, rwkv_k_k, rwkv_k_a, rwkv_r_k, rwkv_gn_w, rwkv_gn_b, w_out, w_up, ffn_conv_w, ffn_conv_b, w_down):
    batch, seq, d = x.shape
    depth = w_in.shape[0]
    assert seq % ROW_TILE == 0 and seq % ATTN_TK == 0 and ROW_TILE % CHUNK == 0
    m = batch * seq
    nc_total = m // CHUNK
    cos_t, sin_t = _rope_tables(seq)
    tri, sel, bd = _scan_constants(ROW_TILE)
    x2 = x.reshape(m, d)
    diff_cols = 3 * DIFF_WIDTH
    row2 = lambda a: a.reshape(1, -1)
    for l in range(depth):
        lambda_init = 0.8 - 0.6 * math.exp(-0.3 * l)
        wqkv = w_in[l][:, :diff_cols].astype(BF16)
        wrw = w_in[l][:, diff_cols:].astype(BF16)
        q0, q1, k, v, rw = _in_proj(x2, row2(pre_mix_norm[l]), wqkv, wrw, cos_t, sin_t, seq)
        lam_p = jnp.stack([diff_lambda_q1[l], diff_lambda_k1[l], diff_lambda_q2[l], diff_lambda_k2[l]])
        ya = _attention(lam_p, diff_subln[l].reshape(-1, 1), q0, q1, k, _values_transposed(v, batch, seq),
                        batch, seq, lambda_init)
        prep = _rwkv_prep(rw, row2(rwkv_mu[l]), rwkv_w0[l], _lora_blockdiag(rwkv_w2[l]), rwkv_a0[l],
                          _lora_blockdiag(rwkv_a2[l]), rwkv_g2[l].astype(BF16), row2(rwkv_k_k[l]),
                          row2(rwkv_k_a[l]), row2(rwkv_r_k[l]), bd, tri, sel, seq)
        vb, g, bonus = prep[0], prep[1], prep[2]
        dir0 = list(prep[3:9]) + [prep[9].reshape(nc_total, 1, RWKV_WIDTH)]
        dir1 = list(prep[10:16]) + [prep[16].reshape(nc_total, 1, RWKV_WIDTH)]
        yf, yb = _wkv(vb, dir0, dir1, batch, seq)
        x2 = _out_proj(x2, ya, yf, yb, bonus, g, row2(rwkv_gn_w[l]), row2(rwkv_gn_b[l]), bd,
                       w_out[l].astype(BF16), row2(post_mix_norm[l]))
        x2 = _ffn(x2, row2(pre_ffn_norm[l]), w_up[l].astype(BF16), ffn_conv_w[l], row2(ffn_conv_b[l]),
                  w_down[l].astype(BF16), row2(post_ffn_norm[l]), seq)
    return x2.reshape(batch, seq, d)
```

```python
import functools
import math

import jax
import jax.numpy as jnp
from jax import lax
from jax.experimental import pallas as pl
from jax.experimental.pallas import tpu as pltpu

F32 = jnp.float32
BF16 = jnp.bfloat16

LANES = 128
SUBLANES = 8
VMEM_LIMIT = 56 * 1024 * 1024

DIFF_QK_DIM = 64
DIFF_V_DIM = 128
DIFF_HEADS = 4
DIFF_WIDTH = DIFF_HEADS * DIFF_V_DIM
RWKV_HEAD_DIM = 64
RWKV_WIDTH = 512
RWKV_PAIRS = RWKV_WIDTH // LANES
DECAY_LORA = 64
ICLR_LORA = 64
GATE_LORA = 128
ROPE_THETA = 10000.0
NORM_EPS = 1e-6
GN_EPS = 64e-5
CHUNK = 64

ROW_TILE = 512
FFN_ROW_TILE = 1024
ATTN_TQ = 256
ATTN_TK = 1024
ATTN_VROWS = 144
FF_CHUNK = 256


def _rms(x, g):
    return x * lax.rsqrt(jnp.mean(x * x, axis=-1, keepdims=True) + NORM_EPS) * g


def _mm(a, b):
    return jnp.dot(a.astype(BF16), b.astype(BF16), preferred_element_type=F32)


def _mm_nt(a, b):
    return lax.dot_general(a.astype(BF16), b.astype(BF16), (((1,), (1,)), ((), ())),
                           preferred_element_type=F32)


def _mm_tn(a, b):
    return lax.dot_general(a.astype(BF16), b.astype(BF16), (((0,), (0,)), ((), ())),
                           preferred_element_type=F32)


def _split2(x):
    hi = x.astype(BF16)
    lo = (x - hi.astype(F32)).astype(BF16)
    return hi, lo


def _split3(x):
    hi = x.astype(BF16)
    r1 = x - hi.astype(F32)
    mid = r1.astype(BF16)
    lo = (r1 - mid.astype(F32)).astype(BF16)
    return hi, mid, lo


def _seg_sum(x, bd):
    hi, lo = _split2(x)
    return (jnp.dot(hi, bd, preferred_element_type=F32)
            + jnp.dot(lo, bd, preferred_element_type=F32))


def _params(semantics):
    return pltpu.CompilerParams(dimension_semantics=semantics, vmem_limit_bytes=VMEM_LIMIT)


def _inproj_kernel(x_ref, g_ref, wqkv_ref, wrw_ref, cos_ref, sin_ref,
                   q0_ref, q1_ref, k_ref, v_ref, rw_ref):
    tm = x_ref.shape[0]
    hb = _rms(x_ref[...], g_ref[...]).astype(BF16)
    qk = jnp.dot(hb, wqkv_ref[:, :2 * DIFF_WIDTH], preferred_element_type=F32)
    cos = cos_ref[...]
    sin = sin_ref[...]
    lane = lax.broadcasted_iota(jnp.int32, (tm, LANES), 1)
    first_half = (lane % DIFF_QK_DIM) < (DIFF_QK_DIM // 2)
    comp0 = lane < DIFF_QK_DIM
    scale = DIFF_QK_DIM ** -0.5
    for j in range(2 * DIFF_HEADS):
        t = qk[:, j * LANES:(j + 1) * LANES]
        partner = jnp.where(first_half, pltpu.roll(t, LANES - DIFF_QK_DIM // 2, 1),
                            pltpu.roll(t, DIFF_QK_DIM // 2, 1))
        o = t * cos + partner * sin
        if j < DIFF_HEADS:
            o = o * scale
            q0_ref[:, j * LANES:(j + 1) * LANES] = jnp.where(comp0, o, 0.0).astype(BF16)
            q1_ref[:, j * LANES:(j + 1) * LANES] = jnp.where(comp0, 0.0, o).astype(BF16)
        else:
            jj = j - DIFF_HEADS
            k_ref[:, jj * LANES:(jj + 1) * LANES] = o.astype(BF16)
    v_ref[...] = jnp.dot(hb, wqkv_ref[:, 2 * DIFF_WIDTH:], preferred_element_type=F32).astype(BF16)
    rw_ref[...] = jnp.dot(hb, wrw_ref[...], preferred_element_type=F32)


def _in_proj(x2, g, wqkv, wrw, cos_t, sin_t, seq):
    m, d = x2.shape
    tm = ROW_TILE
    tps = seq // tm
    rw_cols = wrw.shape[1]
    row = lambda i: (i, 0)
    const = lambda i: (0, 0)
    return pl.pallas_call(
        _inproj_kernel,
        grid=(m // tm,),
        in_specs=[
            pl.BlockSpec((tm, d), row),
            pl.BlockSpec((1, d), const, pipeline_mode=pl.Buffered(1)),
            pl.BlockSpec(wqkv.shape, const, pipeline_mode=pl.Buffered(1)),
            pl.BlockSpec(wrw.shape, const, pipeline_mode=pl.Buffered(1)),
            pl.BlockSpec((tm, LANES), lambda i: (i % tps, 0)),
            pl.BlockSpec((tm, LANES), lambda i: (i % tps, 0)),
        ],
        out_specs=[
            pl.BlockSpec((tm, DIFF_WIDTH), row),
            pl.BlockSpec((tm, DIFF_WIDTH), row),
            pl.BlockSpec((tm, DIFF_WIDTH), row),
            pl.BlockSpec((tm, DIFF_WIDTH), row),
            pl.BlockSpec((tm, rw_cols), row),
        ],
        out_shape=[
            jax.ShapeDtypeStruct((m, DIFF_WIDTH), BF16),
            jax.ShapeDtypeStruct((m, DIFF_WIDTH), BF16),
            jax.ShapeDtypeStruct((m, DIFF_WIDTH), BF16),
            jax.ShapeDtypeStruct((m, DIFF_WIDTH), BF16),
            jax.ShapeDtypeStruct((m, rw_cols), F32),
        ],
        compiler_params=_params(("parallel",)),
        name="in_proj",
    )(x2, g, wqkv, wrw, cos_t, sin_t)


def _attn_kernel(lam_ref, sub_ref, q0_ref, q1_ref, q0n_ref, q1n_ref, k_ref, vt_ref, o_ref,
                 s_sc, mx_sc, m_sc, acc_sc, *, tk, lambda_init):
    seq = k_ref.shape[0]
    nk = seq // tk
    qs = (q0_ref[...], q1_ref[...])

    def scores(qpair, i, slot):
        start = i * tk
        ks = k_ref[pl.ds(start, tk), :]
        maxima = []
        for c in range(2):
            st = lax.dot_general(ks, qpair[c], (((1,), (1,)), ((), ())), preferred_element_type=F32)
            s_sc[slot, c] = st
            maxima.append(jnp.max(st, axis=0, keepdims=True))
        return tuple(maxima)

    @pl.when(pl.program_id(2) == 0)
    def _():
        first = scores(qs, 0, 0)
        mx_sc[0] = first[0]
        mx_sc[1] = first[1]

    m_sc[...] = jnp.full(m_sc.shape, -jnp.inf, F32)
    acc_sc[...] = jnp.zeros(acc_sc.shape, F32)

    def accumulate(i, slot, maxima):
        start = i * tk
        vt = vt_ref[:, pl.ds(start, tk)]
        for c in range(2):
            m_old = m_sc[c]
            m_new = jnp.maximum(m_old, maxima[c])
            alpha = jnp.exp(m_old - m_new)
            p = jnp.exp(s_sc[slot, c] - m_new).astype(BF16)
            acc_sc[c] = alpha * acc_sc[c] + jnp.dot(vt, p, preferred_element_type=F32)
            m_sc[c] = m_new

    maxima = (mx_sc[0], mx_sc[1])
    for blk in range(nk):
        if blk + 1 < nk:
            ahead = scores(qs, blk + 1, (blk + 1) % 2)
        else:
            ahead = scores((q0n_ref[...], q1n_ref[...]), 0, 0)
        accumulate(blk, blk % 2, maxima)
        maxima = ahead
    mx_sc[0] = maxima[0]
    mx_sc[1] = maxima[1]

    lp = lam_ref[...]
    lam = (jnp.exp(jnp.sum(lp[0:1] * lp[1:2], axis=-1, keepdims=True))
           - jnp.exp(jnp.sum(lp[2:3] * lp[3:4], axis=-1, keepdims=True)) + lambda_init)
    a0 = acc_sc[0]
    a1 = acc_sc[1]
    dv = DIFF_V_DIM
    o = a0[:dv] / a0[dv:dv + 1] - lam * (a1[:dv] / a1[dv:dv + 1])
    o = o * lax.rsqrt(jnp.mean(o * o, axis=0, keepdims=True) + NORM_EPS) * sub_ref[...] * (1.0 - lambda_init)
    o_ref[...] = o.T.astype(BF16)


def _attention(lam_p, subln_col, q0, q1, k, vt, batch, seq, lambda_init):
    tq, tk = ATTN_TQ, min(ATTN_TK, seq // 2)
    assert (seq // tk) % 2 == 0
    nq = seq // tq
    qmap = lambda b, h, i: (b * nq + i, h)
    qnext = lambda b, h, i: (b * nq + jnp.minimum(i + 1, nq - 1), h)
    const = lambda b, h, i: (0, 0)
    return pl.pallas_call(
        functools.partial(_attn_kernel, tk=tk, lambda_init=lambda_init),
        grid=(batch, DIFF_HEADS, nq),
        in_specs=[
            pl.BlockSpec(lam_p.shape, const),
            pl.BlockSpec(subln_col.shape, const),
            pl.BlockSpec((tq, LANES), qmap),
            pl.BlockSpec((tq, LANES), qmap),
            pl.BlockSpec((tq, LANES), qnext),
            pl.BlockSpec((tq, LANES), qnext),
            pl.BlockSpec((seq, LANES), lambda b, h, i: (b, h)),
            pl.BlockSpec((ATTN_VROWS, seq), lambda b, h, i: (b * DIFF_HEADS + h, 0)),
        ],
        out_specs=pl.BlockSpec((tq, LANES), qmap),
        out_shape=jax.ShapeDtypeStruct((batch * seq, DIFF_WIDTH), BF16),
        scratch_shapes=[
            pltpu.VMEM((2, 2, tk, tq), F32),
            pltpu.VMEM((2, 1, tq), F32),
            pltpu.VMEM((2, 1, tq), F32),
            pltpu.VMEM((2, ATTN_VROWS, tq), F32),
        ],
        compiler_params=_params(("parallel", "parallel", "arbitrary")),
        name="diff_attn",
    )(lam_p, subln_col, q0, q1, q0, q1, k, vt)


def _values_transposed(v, batch, seq):
    vt = v.reshape(batch, seq, DIFF_HEADS, DIFF_V_DIM).transpose(0, 2, 3, 1)
    pad = jnp.zeros((batch, DIFF_HEADS, ATTN_VROWS - DIFF_V_DIM, seq), v.dtype).at[:, :, 0, :].set(1)
    return jnp.concatenate([vt, pad], axis=2).reshape(batch * DIFF_HEADS * ATTN_VROWS, seq)


def _sigmoid(x):
    return 1.0 / (1.0 + jnp.exp(-x))


def _rwkv_prep_kernel(z_ref, zp_ref, zn_ref, mu_ref, w0_ref, w2_ref, a0_ref, a2_ref, g2_ref,
                      kk_ref, ka_ref, rk_ref, bd_ref, tri_ref,
                      v_o, g_o, bonus_o,
                      at0_o, rt0_o, bb0_o, kb0_o, bh0_o, kh0_o, pl0_o,
                      at1_o, rt1_o, bb1_o, kb1_o, bh1_o, kh1_o, pl1_o, *, tiles_per_seq):
    tm = z_ref.shape[0]
    w = RWKV_WIDTH
    i = pl.program_id(0)
    first = (i % tiles_per_seq) == 0
    last = (i % tiles_per_seq) == tiles_per_seq - 1
    z = z_ref[...]
    prow = jnp.where(first, 0.0, zp_ref[SUBLANES - 1:SUBLANES, :])
    nrow = jnp.where(last, 0.0, zn_ref[0:1, :])
    row = lax.broadcasted_iota(jnp.int32, (tm, 1), 0)
    zp = jnp.where(row == 0, prow, pltpu.roll(z, 1, 0))
    zn = jnp.where(row == tm - 1, nrow, pltpu.roll(z, tm - 1, 0))
    zs = z + (0.5 * (zp + zn) - z) * mu_ref[...]

    r = zs[:, 0:w]
    k = zs[:, w:2 * w]
    v = zs[:, 2 * w:3 * w]
    o3 = 3 * w
    wd = zs[:, o3:o3 + 2 * DECAY_LORA]
    ad = zs[:, o3 + 2 * DECAY_LORA:o3 + 2 * DECAY_LORA + 2 * ICLR_LORA]
    gd = zs[:, o3 + 2 * DECAY_LORA + 2 * ICLR_LORA:]
    bd = bd_ref[...]

    g_o[...] = _mm(_sigmoid(gd), g2_ref[...])
    v_o[...] = v.astype(BF16)
    kkf = k * kk_ref[...]
    kk = kkf * lax.rsqrt(jnp.maximum(_seg_sum(kkf * kkf, bd), 1e-12))
    wl_all = _mm(jnp.tanh(wd), w2_ref[...])
    al_all = _mm(ad, a2_ref[...])
    outs = ((at0_o, rt0_o, bb0_o, kb0_o, bh0_o, kh0_o, pl0_o),
            (at1_o, rt1_o, bb1_o, kb1_o, bh1_o, kh1_o, pl1_o))
    bonus = jnp.zeros((tm, w), F32)
    for d in range(2):
        at_o, rt_o, bb_o, kb_o, bh_o, kh_o, pl_o = outs[d]
        lw = -math.exp(-0.5) * _sigmoid(w0_ref[d:d + 1, :] + wl_all[:, d * w:(d + 1) * w])
        a = _sigmoid(a0_ref[d:d + 1, :] + al_all[:, d * w:(d + 1) * w])
        kd = k * (1.0 + (a - 1.0) * ka_ref[...])
        bonus = bonus + _seg_sum(r * kd * rk_ref[...], bd) * v
        cin = sum(jnp.dot(tri_ref[d], p, preferred_element_type=F32) for p in _split3(lw))
        cin3 = cin.reshape(tm // CHUNK, CHUNK, w)
        tot3 = cin3[:, CHUNK - 1:CHUNK, :] if d == 0 else cin3[:, 0:1, :]
        rem = (tot3 - cin3).reshape(tm, w)
        tot = tot3.reshape(tm // CHUNK, w)
        e_in = jnp.exp(cin)
        e_neg = jnp.exp(-cin)
        e_rem = jnp.exp(rem)
        beta = kk * a
        at_o[...] = (-kk * jnp.exp(cin - lw)).astype(BF16)
        rt_o[...] = (r * e_in).astype(BF16)
        bb_o[...] = (beta * e_neg).astype(BF16)
        kb_o[...] = (kd * e_neg).astype(BF16)
        bh_o[...] = (beta * e_rem).astype(BF16)
        kh_o[...] = (kd * e_rem).astype(BF16)
        pl_o[...] = jnp.exp(tot)
    bonus_o[...] = bonus


def _rwkv_prep(rw, mu, w0, w2big, a0, a2big, g2, k_k, k_a, r_k, bd, tri, seq):
    m, cols = rw.shape
    tm = ROW_TILE
    nblk8 = m // SUBLANES
    w = RWKV_WIDTH
    row = lambda i: (i, 0)
    const2 = lambda i: (0, 0)
    const3 = lambda i: (0, 0, 0)
    full = lambda a: pl.BlockSpec(a.shape, const2 if a.ndim == 2 else const3, pipeline_mode=pl.Buffered(1))
    big_bf = jax.ShapeDtypeStruct((m, w), BF16)
    big_f = jax.ShapeDtypeStruct((m, w), F32)
    pl_shape = jax.ShapeDtypeStruct((m // CHUNK, w), F32)
    per_dir_shapes = [big_bf] * 6 + [pl_shape]
    per_dir_specs = [pl.BlockSpec((tm, w), row)] * 6 + [pl.BlockSpec((tm // CHUNK, w), row)]
    return pl.pallas_call(
        functools.partial(_rwkv_prep_kernel, tiles_per_seq=seq // tm),
        grid=(m // tm,),
        in_specs=[
            pl.BlockSpec((tm, cols), row),
            pl.BlockSpec((SUBLANES, cols), lambda i: (jnp.maximum(i * (tm // SUBLANES) - 1, 0), 0)),
            pl.BlockSpec((SUBLANES, cols), lambda i: (jnp.minimum((i + 1) * (tm // SUBLANES), nblk8 - 1), 0)),
            full(mu), full(w0), full(w2big), full(a0), full(a2big), full(g2),
            full(k_k), full(k_a), full(r_k), full(bd), full(tri),
        ],
        out_specs=[pl.BlockSpec((tm, w), row)] * 3 + per_dir_specs * 2,
        out_shape=[big_bf, big_f, big_f] + per_dir_shapes * 2,
        compiler_params=_params(("parallel",)),
        name="rwkv_prep",
    )(rw, rw, rw, mu, w0, w2big, a0, a2big, g2, k_k, k_a, r_k, bd, tri)


def _wkv_kernel(v0_ref, at0, rt0, bb0, kb0, bh0, kh0, pl0,
                v1_ref, at1, rt1, bb1, kb1, bh1, kh1, pl1,
                y0_ref, y1_ref, s_sc):
    c = pl.program_id(1)

    @pl.when(c == 0)
    def _():
        s_sc[...] = jnp.zeros(s_sc.shape, F32)

    n = 2 * CHUNK
    ri = lax.broadcasted_iota(jnp.int32, (n, n), 0)
    ci = lax.broadcasted_iota(jnp.int32, (n, n), 1)
    ti, tj = ri % CHUNK, ci % CHUNK
    same = (ri // CHUNK) == (ci // CHUNK)
    eye = ri == ci
    head0 = lax.broadcasted_iota(jnp.int32, (CHUNK, LANES), 1) < RWKV_HEAD_DIM

    def stack(x):
        zero = jnp.zeros_like(x)
        return jnp.concatenate([jnp.where(head0, x, zero), jnp.where(head0, zero, x)], axis=0)

    dirs = ((v0_ref, at0, rt0, bb0, kb0, bh0, kh0, pl0, y0_ref),
            (v1_ref, at1, rt1, bb1, kb1, bh1, kh1, pl1, y1_ref))
    units = []
    for d, (v_ref, at_r, rt_r, bb_r, kb_r, bh_r, kh_r, pl_r, y_ref) in enumerate(dirs):
        before = (ti > tj) if d == 0 else (ti < tj)
        strict = same & before
        incl = strict | eye
        levels = []
        b = 1
        while b < CHUNK:
            blk = same & ((ti // (2 * b)) == (tj // (2 * b)))
            hi, hj = (ti // b) % 2, (tj // b) % 2
            levels.append(blk & ((hi == 1) & (hj == 0) if d == 0 else (hi == 0) & (hj == 1)))
            b *= 2
        for p in range(RWKV_PAIRS):
            units.append((d, p, slice(p * LANES, (p + 1) * LANES), strict, incl, levels,
                          (v_ref, at_r, rt_r, bb_r, kb_r, bh_r, kh_r, pl_r, y_ref)))
    nu = range(len(units))

    def stacked(k):
        return [stack(u[6][k][:, u[2]]) for u in units]

    v_s, at_s, rt_s, bb_s, kb_s, bh_s, kh_s = (stacked(k) for k in range(7))
    gram = [_mm_nt(jnp.concatenate([at_s[i], rt_s[i]], axis=0), jnp.concatenate([bb_s[i], kb_s[i]], axis=0))
            for i in nu]
    a_ab = [jnp.where(units[i][3], gram[i][:n, :n], 0.0) for i in nu]
    a_ak = [jnp.where(units[i][3], gram[i][:n, n:], 0.0).astype(BF16) for i in nu]
    a_rb = [jnp.where(units[i][4], gram[i][n:, :n], 0.0).astype(BF16) for i in nu]
    a_rk = [jnp.where(units[i][4], gram[i][n:, n:], 0.0).astype(BF16) for i in nu]
    t_inv = [jnp.where(eye, 1.0, jnp.where(units[i][5][0], a_ab[i], 0.0)) for i in nu]
    for lv in range(1, len(units[0][5])):
        t_b = [t.astype(BF16) for t in t_inv]
        half = [_mm(t_b[i], jnp.where(units[i][5][lv], a_ab[i], 0.0)) for i in nu]
        t_inv = [t_inv[i] + _mm(half[i], t_b[i]) for i in nu]
    t_b = [t.astype(BF16) for t in t_inv]
    av = [_mm(jnp.concatenate([a_ak[i], a_rk[i]], axis=0), v_s[i]) for i in nu]
    akv = [av[i][:n].astype(BF16) for i in nu]
    wu = [_mm(t_b[i], jnp.concatenate([at_s[i], akv[i]], axis=1)).astype(BF16) for i in nu]
    qy = [_mm(a_rb[i], wu[i]) for i in nu]
    y0 = [qy[i][:, LANES:] + av[i][n:] for i in nu]
    qt = [rt_s[i].astype(F32) + qy[i][:, :LANES] for i in nu]
    mn = [_mm_tn(wu[i], bh_s[i]) for i in nu]
    n0t = [mn[i][LANES:, :] + _mm_tn(v_s[i], kh_s[i]) for i in nu]
    for i in nu:
        d, p, sl = units[i][0], units[i][1], units[i][2]
        pl_r, y_ref = units[i][6][7], units[i][6][8]
        st = s_sc[d, p]
        ys = _mm_nt(qt[i], st) + y0[i]
        y_ref[:, sl] = ys[:CHUNK] + ys[CHUNK:]
        s_sc[d, p] = st * pl_r[0, :, sl] + _mm(st, mn[i][:LANES, :]) + n0t[i]


def _wkv(v, dir0, dir1, batch, seq):
    m, w = v.shape
    nc = seq // CHUNK
    fmap = lambda b, c: (b * nc + c, 0)
    bmap = lambda b, c: (b * nc + nc - 1 - c, 0)
    fmap3 = lambda b, c: (b * nc + c, 0, 0)
    bmap3 = lambda b, c: (b * nc + nc - 1 - c, 0, 0)
    blk = lambda im: pl.BlockSpec((CHUNK, w), im)
    in_specs = ([blk(fmap)] * 7 + [pl.BlockSpec((1, 1, w), fmap3)]
                + [blk(bmap)] * 7 + [pl.BlockSpec((1, 1, w), bmap3)])
    return pl.pallas_call(
        _wkv_kernel,
        grid=(batch, nc),
        in_specs=in_specs,
        out_specs=[blk(fmap), blk(bmap)],
        out_shape=[jax.ShapeDtypeStruct((m, w), F32)] * 2,
        scratch_shapes=[pltpu.VMEM((2, RWKV_PAIRS, LANES, LANES), F32)],
        compiler_params=_params(("parallel", "arbitrary")),
        name="wkv_scan",
    )(v, *dir0, v, *dir1)


def _outproj_kernel(x_ref, ya_ref, yf_ref, yb_ref, bonus_ref, g_ref, gnw_ref, gnb_ref, bd_ref,
                    wout_ref, pg_ref, o_ref):
    bd = bd_ref[...]
    inv_n = 1.0 / RWKV_HEAD_DIM
    y = yf_ref[...] + yb_ref[...]
    mean = _seg_sum(y, bd) * inv_n
    yc = y - mean
    var = _seg_sum(yc * yc, bd) * inv_n
    yn = yc * lax.rsqrt(var + GN_EPS) * gnw_ref[...] + gnb_ref[...]
    yb = ((yn + bonus_ref[...]) * g_ref[...]).astype(BF16)
    mix = (jnp.dot(ya_ref[...], wout_ref[:DIFF_WIDTH, :], preferred_element_type=F32)
           + jnp.dot(yb, wout_ref[DIFF_WIDTH:, :], preferred_element_type=F32))
    o_ref[...] = x_ref[...] + _rms(mix, pg_ref[...])


def _out_proj(x2, ya, yf, yb, bonus, g, gn_w, gn_b, bd, wout, post_g):
    m, d = x2.shape
    tm = ROW_TILE
    w = RWKV_WIDTH
    row = lambda i: (i, 0)
    const = lambda i: (0, 0)
    full = lambda a: pl.BlockSpec(a.shape, const, pipeline_mode=pl.Buffered(1))
    return pl.pallas_call(
        _outproj_kernel,
        grid=(m // tm,),
        in_specs=[pl.BlockSpec((tm, d), row), pl.BlockSpec((tm, DIFF_WIDTH), row)]
                 + [pl.BlockSpec((tm, w), row)] * 4
                 + [full(gn_w), full(gn_b), full(bd), full(wout), full(post_g)],
        out_specs=pl.BlockSpec((tm, d), row),
        out_shape=jax.ShapeDtypeStruct((m, d), F32),
        compiler_params=_params(("parallel",)),
        name="out_proj",
    )(x2, ya, yf, yb, bonus, g, gn_w, gn_b, bd, wout, post_g)


def _gelu_tanh(x):
    c = math.sqrt(2.0 / math.pi)
    return 0.5 * x * (1.0 + jnp.tanh(c * (x + 0.044715 * (x * x * x))))


def _ffn_kernel(x_ref, xp_ref, xn_ref, g_ref, wup_ref, cw_ref, cb_ref, wdn_ref, pg_ref, o_ref, act_sc,
                *, tiles_per_seq, d_ff):
    tm = x_ref.shape[0]
    i = pl.program_id(0)
    first = (i % tiles_per_seq) == 0
    last = (i % tiles_per_seq) == tiles_per_seq - 1
    g = g_ref[...]
    x = x_ref[...]
    hb = _rms(x, g).astype(BF16)
    hp = _rms(xp_ref[...], g).astype(BF16)
    hn = _rms(xn_ref[...], g).astype(BF16)
    hext = jnp.concatenate([hp, hb, hn], axis=0)
    ext = tm + 2 * SUBLANES
    row = lax.broadcasted_iota(jnp.int32, (ext, 1), 0)
    kill_prev = jnp.logical_and(first, row == SUBLANES)
    kill_next = jnp.logical_and(last, row == SUBLANES + tm - 1)
    for j in range(d_ff // FF_CHUNK):
        cs = slice(j * FF_CHUNK, (j + 1) * FF_CHUNK)
        gate = jnp.dot(hext, wup_ref[:, cs], preferred_element_type=F32)
        up = jnp.dot(hb, wup_ref[:, d_ff + j * FF_CHUNK:d_ff + (j + 1) * FF_CHUNK],
                     preferred_element_type=F32)
        gp = jnp.where(kill_prev, 0.0, pltpu.roll(gate, 1, 0))
        gn = jnp.where(kill_next, 0.0, pltpu.roll(gate, ext - 1, 0))
        conv = cw_ref[0:1, cs] * gp + cw_ref[1:2, cs] * gate + cw_ref[2:3, cs] * gn + cb_ref[:, cs]
        conv = conv[SUBLANES:SUBLANES + tm, :]
        act_sc[:, cs] = (_gelu_tanh(conv) * up).astype(BF16)
    down = jnp.dot(act_sc[...], wdn_ref[...], preferred_element_type=F32)
    o_ref[...] = x + _rms(down, pg_ref[...])


def _ffn(x1, pre_g, wup, conv_w, conv_b, wdn, post_g, seq):
    m, d = x1.shape
    tm = min(FFN_ROW_TILE, seq)
    d_ff = wdn.shape[0]
    nblk8 = m // SUBLANES
    row = lambda i: (i, 0)
    const = lambda i: (0, 0)
    full = lambda a: pl.BlockSpec(a.shape, const, pipeline_mode=pl.Buffered(1))
    return pl.pallas_call(
        functools.partial(_ffn_kernel, tiles_per_seq=seq // tm, d_ff=d_ff),
        grid=(m // tm,),
        in_specs=[
            pl.BlockSpec((tm, d), row),
            pl.BlockSpec((SUBLANES, d), lambda i: (jnp.maximum(i * (tm // SUBLANES) - 1, 0), 0)),
            pl.BlockSpec((SUBLANES, d), lambda i: (jnp.minimum((i + 1) * (tm // SUBLANES), nblk8 - 1), 0)),
            full(pre_g), full(wup), full(conv_w), full(conv_b), full(wdn), full(post_g),
        ],
        out_specs=pl.BlockSpec((tm, d), row),
        out_shape=jax.ShapeDtypeStruct((m, d), F32),
        scratch_shapes=[pltpu.VMEM((tm, d_ff), BF16)],
        compiler_params=_params(("parallel",)),
        name="conv_ffn",
    )(x1, x1, x1, pre_g, wup, conv_w, conv_b, wdn, post_g)


def _rope_tables(seq):
    half = DIFF_QK_DIM // 2
    pos = jnp.arange(seq, dtype=F32)
    inv = ROPE_THETA ** (-jnp.arange(half, dtype=F32) / half)
    ang = pos[:, None] * inv[None, :]
    cos, sin = jnp.cos(ang), jnp.sin(ang)
    reps = LANES // DIFF_QK_DIM
    cos_t = jnp.tile(jnp.concatenate([cos, cos], axis=-1), (1, reps))
    sin_t = jnp.tile(jnp.concatenate([-sin, sin], axis=-1), (1, reps))
    return cos_t, sin_t


def _scan_constants(tm):
    t = jnp.arange(tm)
    same = (t[:, None] // CHUNK) == (t[None, :] // CHUNK)
    lower = t[:, None] >= t[None, :]
    upper = t[:, None] <= t[None, :]
    tri = jnp.stack([same & lower, same & upper])
    ch = jnp.arange(RWKV_WIDTH)
    bd = (ch[:, None] // RWKV_HEAD_DIM) == (ch[None, :] // RWKV_HEAD_DIM)
    return tri.astype(BF16), bd.astype(BF16)


def _lora_blockdiag(w2):
    z = jnp.zeros_like(w2[0])
    return jnp.concatenate([jnp.concatenate([w2[0], z], axis=1),
                            jnp.concatenate([z, w2[1]], axis=1)], axis=0).astype(BF16)


def kernel(x, pre_mix_norm, post_mix_norm, pre_ffn_norm, post_ffn_norm, w_in, diff_lambda_q1, diff_lambda_k1, diff_lambda_q2, diff_lambda_k2, diff_subln, rwkv_mu, rwkv_w0, rwkv_w2, rwkv_a0, rwkv_a2, rwkv_g2, rwkv_k_k, rwkv_k_a, rwkv_r_k, rwkv_gn_w, rwkv_gn_b, w_out, w_up, ffn_conv_w, ffn_conv_b, w_down):
    batch, seq, d = x.shape
    depth = w_in.shape[0]
    assert seq % ROW_TILE == 0 and seq % ATTN_TK == 0 and ROW_TILE % CHUNK == 0
    m = batch * seq
    nc_total = m // CHUNK
    cos_t, sin_t = _rope_tables(seq)
    tri, bd = _scan_constants(ROW_TILE)
    x2 = x.reshape(m, d)
    diff_cols = 3 * DIFF_WIDTH
    row2 = lambda a: a.reshape(1, -1)
    for l in range(depth):
        lambda_init = 0.8 - 0.6 * math.exp(-0.3 * l)
        wqkv = w_in[l][:, :diff_cols].astype(BF16)
        wrw = w_in[l][:, diff_cols:].astype(BF16)
        q0, q1, k, v, rw = _in_proj(x2, row2(pre_mix_norm[l]), wqkv, wrw, cos_t, sin_t, seq)
        lam_p = jnp.stack([diff_lambda_q1[l], diff_lambda_k1[l], diff_lambda_q2[l], diff_lambda_k2[l]])
        ya = _attention(lam_p, diff_subln[l].reshape(-1, 1), q0, q1, k, _values_transposed(v, batch, seq),
                        batch, seq, lambda_init)
        prep = _rwkv_prep(rw, row2(rwkv_mu[l]), rwkv_w0[l], _lora_blockdiag(rwkv_w2[l]), rwkv_a0[l],
                          _lora_blockdiag(rwkv_a2[l]), rwkv_g2[l].astype(BF16), row2(rwkv_k_k[l]),
                          row2(rwkv_k_a[l]), row2(rwkv_r_k[l]), bd, tri, seq)
        vb, g, bonus = prep[0], prep[1], prep[2]
        dir0 = list(prep[3:9]) + [prep[9].reshape(nc_total, 1, RWKV_WIDTH)]
        dir1 = list(prep[10:16]) + [prep[16].reshape(nc_total, 1, RWKV_WIDTH)]
        yf, yb = _wkv(vb, dir0, dir1, batch, seq)
        x2 = _out_proj(x2, ya, yf, yb, bonus, g, row2(rwkv_gn_w[l]), row2(rwkv_gn_b[l]), bd,
                       w_out[l].astype(BF16), row2(post_mix_norm[l]))
        x2 = _ffn(x2, row2(pre_ffn_norm[l]), w_up[l].astype(BF16), ffn_conv_w[l], row2(ffn_conv_b[l]),
                  w_down[l].astype(BF16), row2(post_ffn_norm[l]), seq)
    return x2.reshape(batch, seq, d)
```

```python
import functools
import math

import jax
import jax.numpy as jnp
from jax import lax
from jax.experimental import pallas as pl
from jax.experimental.pallas import tpu as pltpu

F32 = jnp.float32
BF16 = jnp.bfloat16

LANES = 128
SUBLANES = 8
VMEM_LIMIT = 56 * 1024 * 1024

DIFF_QK_DIM = 64
DIFF_V_DIM = 128
DIFF_HEADS = 4
DIFF_WIDTH = DIFF_HEADS * DIFF_V_DIM
RWKV_HEAD_DIM = 64
RWKV_WIDTH = 512
RWKV_PAIRS = RWKV_WIDTH // LANES
DECAY_LORA = 64
ICLR_LORA = 64
GATE_LORA = 128
ROPE_THETA = 10000.0
NORM_EPS = 1e-6
GN_EPS = 64e-5
CHUNK = 64
WKV_SUB = 2

ROW_TILE = 512
FFN_ROW_TILE = 1024
ATTN_TQ = 256
ATTN_TK = 1024
ATTN_VROWS = 144
FF_CHUNK = 256


def _rms(x, g):
    return x * lax.rsqrt(jnp.mean(x * x, axis=-1, keepdims=True) + NORM_EPS) * g


def _mm(a, b):
    return jnp.dot(a.astype(BF16), b.astype(BF16), preferred_element_type=F32)


def _mm_nt(a, b):
    return lax.dot_general(a.astype(BF16), b.astype(BF16), (((1,), (1,)), ((), ())),
                           preferred_element_type=F32)


def _mm_tn(a, b):
    return lax.dot_general(a.astype(BF16), b.astype(BF16), (((0,), (0,)), ((), ())),
                           preferred_element_type=F32)


def _split2(x):
    hi = x.astype(BF16)
    lo = (x - hi.astype(F32)).astype(BF16)
    return hi, lo


def _split3(x):
    hi = x.astype(BF16)
    r1 = x - hi.astype(F32)
    mid = r1.astype(BF16)
    lo = (r1 - mid.astype(F32)).astype(BF16)
    return hi, mid, lo


def _seg_sum(x, bd):
    hi, lo = _split2(x)
    return (jnp.dot(hi, bd, preferred_element_type=F32)
            + jnp.dot(lo, bd, preferred_element_type=F32))


def _params(semantics):
    return pltpu.CompilerParams(dimension_semantics=semantics, vmem_limit_bytes=VMEM_LIMIT)


def _inproj_kernel(x_ref, g_ref, wqk_ref, wvt_ref, wrw_ref, cos_ref, sin_ref,
                   q0_ref, q1_ref, k_ref, vt_ref, rw_ref):
    tm = x_ref.shape[0]
    hb = _rms(x_ref[...], g_ref[...]).astype(BF16)
    qk = jnp.dot(hb, wqk_ref[...], preferred_element_type=F32)
    cos = cos_ref[...]
    sin = sin_ref[...]
    lane = lax.broadcasted_iota(jnp.int32, (tm, LANES), 1)
    first_half = (lane % DIFF_QK_DIM) < (DIFF_QK_DIM // 2)
    comp0 = lane < DIFF_QK_DIM
    scale = DIFF_QK_DIM ** -0.5
    for j in range(2 * DIFF_HEADS):
        t = qk[:, j * LANES:(j + 1) * LANES]
        partner = jnp.where(first_half, pltpu.roll(t, LANES - DIFF_QK_DIM // 2, 1),
                            pltpu.roll(t, DIFF_QK_DIM // 2, 1))
        o = t * cos + partner * sin
        if j < DIFF_HEADS:
            o = o * scale
            q0_ref[:, j * LANES:(j + 1) * LANES] = jnp.where(comp0, o, 0.0).astype(BF16)
            q1_ref[:, j * LANES:(j + 1) * LANES] = jnp.where(comp0, 0.0, o).astype(BF16)
        else:
            jj = j - DIFF_HEADS
            k_ref[:, jj * LANES:(jj + 1) * LANES] = o.astype(BF16)
    vt = lax.dot_general(wvt_ref[...], hb, (((1,), (1,)), ((), ())), preferred_element_type=F32).astype(BF16)
    pad_rows = ATTN_VROWS - DIFF_V_DIM
    ones_row = lax.broadcasted_iota(jnp.int32, (pad_rows, tm), 0) == 0
    pad = jnp.where(ones_row, 1.0, 0.0).astype(BF16)
    for h in range(DIFF_HEADS):
        vt_ref[h * ATTN_VROWS:h * ATTN_VROWS + DIFF_V_DIM, :] = vt[h * DIFF_V_DIM:(h + 1) * DIFF_V_DIM]
        vt_ref[h * ATTN_VROWS + DIFF_V_DIM:(h + 1) * ATTN_VROWS, :] = pad
    rw_ref[...] = jnp.dot(hb, wrw_ref[...], preferred_element_type=F32)


def _in_proj(x2, g, wqk, wvt, wrw, cos_t, sin_t, seq):
    m, d = x2.shape
    tm = ROW_TILE
    tps = seq // tm
    rw_cols = wrw.shape[1]
    row = lambda i: (i, 0)
    const = lambda i: (0, 0)
    return pl.pallas_call(
        _inproj_kernel,
        grid=(m // tm,),
        in_specs=[
            pl.BlockSpec((tm, d), row),
            pl.BlockSpec((1, d), const, pipeline_mode=pl.Buffered(1)),
            pl.BlockSpec(wqk.shape, const, pipeline_mode=pl.Buffered(1)),
            pl.BlockSpec(wvt.shape, const, pipeline_mode=pl.Buffered(1)),
            pl.BlockSpec(wrw.shape, const, pipeline_mode=pl.Buffered(1)),
            pl.BlockSpec((tm, LANES), lambda i: (i % tps, 0)),
            pl.BlockSpec((tm, LANES), lambda i: (i % tps, 0)),
        ],
        out_specs=[
            pl.BlockSpec((tm, DIFF_WIDTH), row),
            pl.BlockSpec((tm, DIFF_WIDTH), row),
            pl.BlockSpec((tm, DIFF_WIDTH), row),
            pl.BlockSpec((DIFF_HEADS * ATTN_VROWS, tm), lambda i: (i // tps, i % tps)),
            pl.BlockSpec((tm, rw_cols), row),
        ],
        out_shape=[
            jax.ShapeDtypeStruct((m, DIFF_WIDTH), BF16),
            jax.ShapeDtypeStruct((m, DIFF_WIDTH), BF16),
            jax.ShapeDtypeStruct((m, DIFF_WIDTH), BF16),
            jax.ShapeDtypeStruct((m // seq * DIFF_HEADS * ATTN_VROWS, seq), BF16),
            jax.ShapeDtypeStruct((m, rw_cols), F32),
        ],
        compiler_params=_params(("parallel",)),
        name="in_proj",
    )(x2, g, wqk, wvt, wrw, cos_t, sin_t)


def _attn_kernel(lam_ref, sub_ref, q0_ref, q1_ref, q0n_ref, q1n_ref, k_ref, vt_ref, o_ref,
                 s_sc, mx_sc, m_sc, acc_sc, *, tk, lambda_init):
    seq = k_ref.shape[0]
    nk = seq // tk
    qs = (q0_ref[...], q1_ref[...])

    def scores(qpair, i, slot):
        start = i * tk
        ks = k_ref[pl.ds(start, tk), :]
        maxima = []
        for c in range(2):
            st = lax.dot_general(ks, qpair[c], (((1,), (1,)), ((), ())), preferred_element_type=F32)
            s_sc[slot, c] = st
            maxima.append(jnp.max(st, axis=0, keepdims=True))
        return tuple(maxima)

    @pl.when(pl.program_id(2) == 0)
    def _():
        first = scores(qs, 0, 0)
        mx_sc[0] = first[0]
        mx_sc[1] = first[1]

    m_sc[...] = jnp.full(m_sc.shape, -jnp.inf, F32)
    acc_sc[...] = jnp.zeros(acc_sc.shape, F32)

    def accumulate(i, slot, maxima):
        start = i * tk
        vt = vt_ref[:, pl.ds(start, tk)]
        for c in range(2):
            m_old = m_sc[c]
            m_new = jnp.maximum(m_old, maxima[c])
            alpha = jnp.exp(m_old - m_new)
            p = jnp.exp(s_sc[slot, c] - m_new).astype(BF16)
            acc_sc[c] = alpha * acc_sc[c] + jnp.dot(vt, p, preferred_element_type=F32)
            m_sc[c] = m_new

    maxima = (mx_sc[0], mx_sc[1])
    for blk in range(nk):
        if blk + 1 < nk:
            ahead = scores(qs, blk + 1, (blk + 1) % 2)
        else:
            ahead = scores((q0n_ref[...], q1n_ref[...]), 0, 0)
        accumulate(blk, blk % 2, maxima)
        maxima = ahead
    mx_sc[0] = maxima[0]
    mx_sc[1] = maxima[1]

    lp = lam_ref[...]
    lam = (jnp.exp(jnp.sum(lp[0:1] * lp[1:2], axis=-1, keepdims=True))
           - jnp.exp(jnp.sum(lp[2:3] * lp[3:4], axis=-1, keepdims=True)) + lambda_init)
    a0 = acc_sc[0]
    a1 = acc_sc[1]
    dv = DIFF_V_DIM
    o = a0[:dv] / a0[dv:dv + 1] - lam * (a1[:dv] / a1[dv:dv + 1])
    o = o * lax.rsqrt(jnp.mean(o * o, axis=0, keepdims=True) + NORM_EPS) * sub_ref[...] * (1.0 - lambda_init)
    o_ref[...] = o.T.astype(BF16)


def _attention(lam_p, subln_col, q0, q1, k, vt, batch, seq, lambda_init):
    tq, tk = ATTN_TQ, min(ATTN_TK, seq // 2)
    assert (seq // tk) % 2 == 0
    nq = seq // tq
    qmap = lambda b, h, i: (b * nq + i, h)
    qnext = lambda b, h, i: (b * nq + jnp.minimum(i + 1, nq - 1), h)
    const = lambda b, h, i: (0, 0)
    return pl.pallas_call(
        functools.partial(_attn_kernel, tk=tk, lambda_init=lambda_init),
        grid=(batch, DIFF_HEADS, nq),
        in_specs=[
            pl.BlockSpec(lam_p.shape, const),
            pl.BlockSpec(subln_col.shape, const),
            pl.BlockSpec((tq, LANES), qmap),
            pl.BlockSpec((tq, LANES), qmap),
            pl.BlockSpec((tq, LANES), qnext),
            pl.BlockSpec((tq, LANES), qnext),
            pl.BlockSpec((seq, LANES), lambda b, h, i: (b, h)),
            pl.BlockSpec((ATTN_VROWS, seq), lambda b, h, i: (b * DIFF_HEADS + h, 0)),
        ],
        out_specs=pl.BlockSpec((tq, LANES), qmap),
        out_shape=jax.ShapeDtypeStruct((batch * seq, DIFF_WIDTH), BF16),
        scratch_shapes=[
            pltpu.VMEM((2, 2, tk, tq), F32),
            pltpu.VMEM((2, 1, tq), F32),
            pltpu.VMEM((2, 1, tq), F32),
            pltpu.VMEM((2, ATTN_VROWS, tq), F32),
        ],
        compiler_params=_params(("parallel", "parallel", "arbitrary")),
        name="diff_attn",
    )(lam_p, subln_col, q0, q1, q0, q1, k, vt)


def _sigmoid(x):
    return 1.0 / (1.0 + jnp.exp(-x))


def _rwkv_prep_kernel(z_ref, zp_ref, zn_ref, mu_ref, w0_ref, w2_ref, a0_ref, a2_ref, g2_ref,
                      kk_ref, ka_ref, rk_ref, bd_ref, tri_ref,
                      v_o, g_o, bonus_o,
                      at0_o, rt0_o, bb0_o, kb0_o, bh0_o, kh0_o, pl0_o,
                      at1_o, rt1_o, bb1_o, kb1_o, bh1_o, kh1_o, pl1_o, *, tiles_per_seq):
    tm = z_ref.shape[0]
    w = RWKV_WIDTH
    i = pl.program_id(0)
    first = (i % tiles_per_seq) == 0
    last = (i % tiles_per_seq) == tiles_per_seq - 1
    z = z_ref[...]
    prow = jnp.where(first, 0.0, zp_ref[SUBLANES - 1:SUBLANES, :])
    nrow = jnp.where(last, 0.0, zn_ref[0:1, :])
    row = lax.broadcasted_iota(jnp.int32, (tm, 1), 0)
    zp = jnp.where(row == 0, prow, pltpu.roll(z, 1, 0))
    zn = jnp.where(row == tm - 1, nrow, pltpu.roll(z, tm - 1, 0))
    zs = z + (0.5 * (zp + zn) - z) * mu_ref[...]

    r = zs[:, 0:w]
    k = zs[:, w:2 * w]
    v = zs[:, 2 * w:3 * w]
    o3 = 3 * w
    wd = zs[:, o3:o3 + 2 * DECAY_LORA]
    ad = zs[:, o3 + 2 * DECAY_LORA:o3 + 2 * DECAY_LORA + 2 * ICLR_LORA]
    gd = zs[:, o3 + 2 * DECAY_LORA + 2 * ICLR_LORA:]
    bd = bd_ref[...]

    g_o[...] = _mm(_sigmoid(gd), g2_ref[...])
    v_o[...] = v.astype(BF16)
    kkf = k * kk_ref[...]
    kk = kkf * lax.rsqrt(jnp.maximum(_seg_sum(kkf * kkf, bd), 1e-12))
    wl_all = _mm(jnp.tanh(wd), w2_ref[...])
    al_all = _mm(ad, a2_ref[...])
    outs = ((at0_o, rt0_o, bb0_o, kb0_o, bh0_o, kh0_o, pl0_o),
            (at1_o, rt1_o, bb1_o, kb1_o, bh1_o, kh1_o, pl1_o))
    bonus = jnp.zeros((tm, w), F32)
    for d in range(2):
        at_o, rt_o, bb_o, kb_o, bh_o, kh_o, pl_o = outs[d]
        lw = -math.exp(-0.5) * _sigmoid(w0_ref[d:d + 1, :] + wl_all[:, d * w:(d + 1) * w])
        a = _sigmoid(a0_ref[d:d + 1, :] + al_all[:, d * w:(d + 1) * w])
        kd = k * (1.0 + (a - 1.0) * ka_ref[...])
        bonus = bonus + _seg_sum(r * kd * rk_ref[...], bd) * v
        cin = sum(jnp.dot(tri_ref[d], p, preferred_element_type=F32) for p in _split3(lw))
        cin3 = cin.reshape(tm // CHUNK, CHUNK, w)
        tot3 = cin3[:, CHUNK - 1:CHUNK, :] if d == 0 else cin3[:, 0:1, :]
        rem = (tot3 - cin3).reshape(tm, w)
        tot = tot3.reshape(tm // CHUNK, w)
        e_in = jnp.exp(cin)
        e_neg = jnp.exp(-cin)
        e_rem = jnp.exp(rem)
        beta = kk * a
        at_o[...] = (-kk * jnp.exp(cin - lw)).astype(BF16)
        rt_o[...] = (r * e_in).astype(BF16)
        bb_o[...] = (beta * e_neg).astype(BF16)
        kb_o[...] = (kd * e_neg).astype(BF16)
        bh_o[...] = (beta * e_rem).astype(BF16)
        kh_o[...] = (kd * e_rem).astype(BF16)
        pl_o[...] = jnp.exp(tot)
    bonus_o[...] = bonus


def _rwkv_prep(rw, mu, w0, w2big, a0, a2big, g2, k_k, k_a, r_k, bd, tri, seq):
    m, cols = rw.shape
    tm = ROW_TILE
    nblk8 = m // SUBLANES
    w = RWKV_WIDTH
    row = lambda i: (i, 0)
    const2 = lambda i: (0, 0)
    const3 = lambda i: (0, 0, 0)
    full = lambda a: pl.BlockSpec(a.shape, const2 if a.ndim == 2 else const3, pipeline_mode=pl.Buffered(1))
    big_bf = jax.ShapeDtypeStruct((m, w), BF16)
    big_f = jax.ShapeDtypeStruct((m, w), F32)
    pl_shape = jax.ShapeDtypeStruct((m // CHUNK, w), F32)
    per_dir_shapes = [big_bf] * 6 + [pl_shape]
    per_dir_specs = [pl.BlockSpec((tm, w), row)] * 6 + [pl.BlockSpec((tm // CHUNK, w), row)]
    return pl.pallas_call(
        functools.partial(_rwkv_prep_kernel, tiles_per_seq=seq // tm),
        grid=(m // tm,),
        in_specs=[
            pl.BlockSpec((tm, cols), row),
            pl.BlockSpec((SUBLANES, cols), lambda i: (jnp.maximum(i * (tm // SUBLANES) - 1, 0), 0)),
            pl.BlockSpec((SUBLANES, cols), lambda i: (jnp.minimum((i + 1) * (tm // SUBLANES), nblk8 - 1), 0)),
            full(mu), full(w0), full(w2big), full(a0), full(a2big), full(g2),
            full(k_k), full(k_a), full(r_k), full(bd), full(tri),
        ],
        out_specs=[pl.BlockSpec((tm, w), row)] * 3 + per_dir_specs * 2,
        out_shape=[big_bf, big_f, big_f] + per_dir_shapes * 2,
        compiler_params=_params(("parallel",)),
        name="rwkv_prep",
    )(rw, rw, rw, mu, w0, w2big, a0, a2big, g2, k_k, k_a, r_k, bd, tri)


def _wkv_kernel(v0_ref, at0, rt0, bb0, kb0, bh0, kh0, pl0,
                v1_ref, at1, rt1, bb1, kb1, bh1, kh1, pl1,
                y0_ref, y1_ref, s_sc):
    c = pl.program_id(1)

    @pl.when(c == 0)
    def _():
        s_sc[...] = jnp.zeros(s_sc.shape, F32)

    n = 2 * CHUNK
    ri = lax.broadcasted_iota(jnp.int32, (n, n), 0)
    ci = lax.broadcasted_iota(jnp.int32, (n, n), 1)
    ti, tj = ri % CHUNK, ci % CHUNK
    same = (ri // CHUNK) == (ci // CHUNK)
    eye = ri == ci
    head0 = lax.broadcasted_iota(jnp.int32, (CHUNK, LANES), 1) < RWKV_HEAD_DIM

    def stack(x):
        zero = jnp.zeros_like(x)
        return jnp.concatenate([jnp.where(head0, x, zero), jnp.where(head0, zero, x)], axis=0)

    dirs = ((v0_ref, at0, rt0, bb0, kb0, bh0, kh0, pl0, y0_ref),
            (v1_ref, at1, rt1, bb1, kb1, bh1, kh1, pl1, y1_ref))
    units = []
    for d, (v_ref, at_r, rt_r, bb_r, kb_r, bh_r, kh_r, pl_r, y_ref) in enumerate(dirs):
        before = (ti > tj) if d == 0 else (ti < tj)
        strict = same & before
        incl = strict | eye
        levels = []
        b = 1
        while b < CHUNK:
            blk = same & ((ti // (2 * b)) == (tj // (2 * b)))
            hi, hj = (ti // b) % 2, (tj // b) % 2
            levels.append(blk & ((hi == 1) & (hj == 0) if d == 0 else (hi == 0) & (hj == 1)))
            b *= 2
        for p in range(RWKV_PAIRS):
            for j in (range(WKV_SUB) if d == 0 else range(WKV_SUB - 1, -1, -1)):
                units.append((d, p, slice(p * LANES, (p + 1) * LANES), strict, incl, levels,
                              (v_ref, at_r, rt_r, bb_r, kb_r, bh_r, kh_r, pl_r, y_ref),
                              j, slice(j * CHUNK, (j + 1) * CHUNK)))
    nu = range(len(units))

    def stacked(k):
        return [stack(u[6][k][u[8], u[2]]) for u in units]

    v_s, at_s, rt_s, bb_s, kb_s, bh_s, kh_s = (stacked(k) for k in range(7))
    gram = [_mm_nt(jnp.concatenate([at_s[i], rt_s[i]], axis=0), jnp.concatenate([bb_s[i], kb_s[i]], axis=0))
            for i in nu]
    a_ab = [jnp.where(units[i][3], gram[i][:n, :n], 0.0) for i in nu]
    a_ak = [jnp.where(units[i][3], gram[i][:n, n:], 0.0).astype(BF16) for i in nu]
    a_rb = [jnp.where(units[i][4], gram[i][n:, :n], 0.0).astype(BF16) for i in nu]
    a_rk = [jnp.where(units[i][4], gram[i][n:, n:], 0.0).astype(BF16) for i in nu]
    t_inv = [jnp.where(eye, 1.0, jnp.where(units[i][5][0], a_ab[i], 0.0)) for i in nu]
    for lv in range(1, len(units[0][5])):
        t_b = [t.astype(BF16) for t in t_inv]
        half = [_mm(t_b[i], jnp.where(units[i][5][lv], a_ab[i], 0.0)) for i in nu]
        t_inv = [t_inv[i] + _mm(half[i], t_b[i]) for i in nu]
    t_b = [t.astype(BF16) for t in t_inv]
    av = [_mm(jnp.concatenate([a_ak[i], a_rk[i]], axis=0), v_s[i]) for i in nu]
    akv = [av[i][:n].astype(BF16) for i in nu]
    wu = [_mm(t_b[i], jnp.concatenate([at_s[i], akv[i]], axis=1)).astype(BF16) for i in nu]
    qy = [_mm(a_rb[i], wu[i]) for i in nu]
    y0 = [qy[i][:, LANES:] + av[i][n:] for i in nu]
    qt = [rt_s[i].astype(F32) + qy[i][:, :LANES] for i in nu]
    mn = [_mm_tn(wu[i], bh_s[i]) for i in nu]
    n0t = [mn[i][LANES:, :] + _mm_tn(v_s[i], kh_s[i]) for i in nu]
    state = {(d, p): s_sc[d, p] for d in range(2) for p in range(RWKV_PAIRS)}
    for k in range(WKV_SUB):
        for i in nu:
            if i % WKV_SUB != k:
                continue
            d, p, sl, rows = units[i][0], units[i][1], units[i][2], units[i][8]
            pl_r, y_ref = units[i][6][7], units[i][6][8]
            st = state[d, p]
            ys = _mm_nt(qt[i], st) + y0[i]
            y_ref[rows, sl] = ys[:CHUNK] + ys[CHUNK:]
            state[d, p] = st * pl_r[units[i][7], :, sl] + _mm(st, mn[i][:LANES, :]) + n0t[i]
    for (d, p), st in state.items():
        s_sc[d, p] = st


def _wkv(v, dir0, dir1, batch, seq):
    m, w = v.shape
    nc = seq // (CHUNK * WKV_SUB)
    fmap = lambda b, c: (b * nc + c, 0)
    bmap = lambda b, c: (b * nc + nc - 1 - c, 0)
    fmap3 = lambda b, c: (b * nc + c, 0, 0)
    bmap3 = lambda b, c: (b * nc + nc - 1 - c, 0, 0)
    blk = lambda im: pl.BlockSpec((CHUNK * WKV_SUB, w), im)
    in_specs = ([blk(fmap)] * 7 + [pl.BlockSpec((WKV_SUB, 1, w), fmap3)]
                + [blk(bmap)] * 7 + [pl.BlockSpec((WKV_SUB, 1, w), bmap3)])
    return pl.pallas_call(
        _wkv_kernel,
        grid=(batch, nc),
        in_specs=in_specs,
        out_specs=[blk(fmap), blk(bmap)],
        out_shape=[jax.ShapeDtypeStruct((m, w), F32)] * 2,
        scratch_shapes=[pltpu.VMEM((2, RWKV_PAIRS, LANES, LANES), F32)],
        compiler_params=_params(("parallel", "arbitrary")),
        name="wkv_scan",
    )(v, *dir0, v, *dir1)


def _outproj_kernel(x_ref, ya_ref, yf_ref, yb_ref, bonus_ref, g_ref, gnw_ref, gnb_ref, bd_ref,
                    wout_ref, pg_ref, o_ref):
    bd = bd_ref[...]
    inv_n = 1.0 / RWKV_HEAD_DIM
    y = yf_ref[...] + yb_ref[...]
    mean = _seg_sum(y, bd) * inv_n
    yc = y - mean
    var = _seg_sum(yc * yc, bd) * inv_n
    yn = yc * lax.rsqrt(var + GN_EPS) * gnw_ref[...] + gnb_ref[...]
    yb = ((yn + bonus_ref[...]) * g_ref[...]).astype(BF16)
    mix = jnp.dot(jnp.concatenate([ya_ref[...], yb], axis=1), wout_ref[...], preferred_element_type=F32)
    o_ref[...] = x_ref[...] + _rms(mix, pg_ref[...])


def _out_proj(x2, ya, yf, yb, bonus, g, gn_w, gn_b, bd, wout, post_g):
    m, d = x2.shape
    tm = ROW_TILE
    w = RWKV_WIDTH
    row = lambda i: (i, 0)
    const = lambda i: (0, 0)
    full = lambda a: pl.BlockSpec(a.shape, const, pipeline_mode=pl.Buffered(1))
    return pl.pallas_call(
        _outproj_kernel,
        grid=(m // tm,),
        in_specs=[pl.BlockSpec((tm, d), row), pl.BlockSpec((tm, DIFF_WIDTH), row)]
                 + [pl.BlockSpec((tm, w), row)] * 4
                 + [full(gn_w), full(gn_b), full(bd), full(wout), full(post_g)],
        out_specs=pl.BlockSpec((tm, d), row),
        out_shape=jax.ShapeDtypeStruct((m, d), F32),
        compiler_params=_params(("parallel",)),
        name="out_proj",
    )(x2, ya, yf, yb, bonus, g, gn_w, gn_b, bd, wout, post_g)


def _gelu_tanh(x):
    c = math.sqrt(2.0 / math.pi)
    return 0.5 * x * (1.0 + jnp.tanh(c * (x + 0.044715 * (x * x * x))))


def _ffn_kernel(x_ref, xp_ref, xn_ref, g_ref, wup_ref, cw_ref, cb_ref, wdn_ref, pg_ref, o_ref, act_sc,
                *, tiles_per_seq, d_ff):
    tm = x_ref.shape[0]
    i = pl.program_id(0)
    first = (i % tiles_per_seq) == 0
    last = (i % tiles_per_seq) == tiles_per_seq - 1
    g = g_ref[...]
    x = x_ref[...]
    hb = _rms(x, g).astype(BF16)
    hp = _rms(xp_ref[...], g).astype(BF16)
    hn = _rms(xn_ref[...], g).astype(BF16)
    hext = jnp.concatenate([hp, hb, hn], axis=0)
    ext = tm + 2 * SUBLANES
    row = lax.broadcasted_iota(jnp.int32, (ext, 1), 0)
    kill_prev = jnp.logical_and(first, row == SUBLANES)
    kill_next = jnp.logical_and(last, row == SUBLANES + tm - 1)
    for j in range(d_ff // FF_CHUNK):
        cs = slice(j * FF_CHUNK, (j + 1) * FF_CHUNK)
        gate = jnp.dot(hext, wup_ref[:, cs], preferred_element_type=F32)
        up = jnp.dot(hb, wup_ref[:, d_ff + j * FF_CHUNK:d_ff + (j + 1) * FF_CHUNK],
                     preferred_element_type=F32)
        gp = jnp.where(kill_prev, 0.0, pltpu.roll(gate, 1, 0))
        gn = jnp.where(kill_next, 0.0, pltpu.roll(gate, ext - 1, 0))
        conv = cw_ref[0:1, cs] * gp + cw_ref[1:2, cs] * gate + cw_ref[2:3, cs] * gn + cb_ref[:, cs]
        conv = conv[SUBLANES:SUBLANES + tm, :]
        act_sc[:, cs] = (_gelu_tanh(conv) * up).astype(BF16)
    down = jnp.dot(act_sc[...], wdn_ref[...], preferred_element_type=F32)
    o_ref[...] = x + _rms(down, pg_ref[...])


def _ffn(x1, pre_g, wup, conv_w, conv_b, wdn, post_g, seq):
    m, d = x1.shape
    tm = min(FFN_ROW_TILE, seq)
    d_ff = wdn.shape[0]
    nblk8 = m // SUBLANES
    row = lambda i: (i, 0)
    const = lambda i: (0, 0)
    full = lambda a: pl.BlockSpec(a.shape, const, pipeline_mode=pl.Buffered(1))
    return pl.pallas_call(
        functools.partial(_ffn_kernel, tiles_per_seq=seq // tm, d_ff=d_ff),
        grid=(m // tm,),
        in_specs=[
            pl.BlockSpec((tm, d), row),
            pl.BlockSpec((SUBLANES, d), lambda i: (jnp.maximum(i * (tm // SUBLANES) - 1, 0), 0)),
            pl.BlockSpec((SUBLANES, d), lambda i: (jnp.minimum((i + 1) * (tm // SUBLANES), nblk8 - 1), 0)),
            full(pre_g), full(wup), full(conv_w), full(conv_b), full(wdn), full(post_g),
        ],
        out_specs=pl.BlockSpec((tm, d), row),
        out_shape=jax.ShapeDtypeStruct((m, d), F32),
        scratch_shapes=[pltpu.VMEM((tm, d_ff), BF16)],
        compiler_params=_params(("parallel",)),
        name="conv_ffn",
    )(x1, x1, x1, pre_g, wup, conv_w, conv_b, wdn, post_g)


def _rope_tables(seq):
    half = DIFF_QK_DIM // 2
    pos = jnp.arange(seq, dtype=F32)
    inv = ROPE_THETA ** (-jnp.arange(half, dtype=F32) / half)
    ang = pos[:, None] * inv[None, :]
    cos, sin = jnp.cos(ang), jnp.sin(ang)
    reps = LANES // DIFF_QK_DIM
    cos_t = jnp.tile(jnp.concatenate([cos, cos], axis=-1), (1, reps))
    sin_t = jnp.tile(jnp.concatenate([-sin, sin], axis=-1), (1, reps))
    return cos_t, sin_t


def _scan_constants(tm):
    t = jnp.arange(tm)
    same = (t[:, None] // CHUNK) == (t[None, :] // CHUNK)
    lower = t[:, None] >= t[None, :]
    upper = t[:, None] <= t[None, :]
    tri = jnp.stack([same & lower, same & upper])
    ch = jnp.arange(RWKV_WIDTH)
    bd = (ch[:, None] // RWKV_HEAD_DIM) == (ch[None, :] // RWKV_HEAD_DIM)
    return tri.astype(BF16), bd.astype(BF16)


def _lora_blockdiag(w2):
    z = jnp.zeros_like(w2[0])
    return jnp.concatenate([jnp.concatenate([w2[0], z], axis=1),
                            jnp.concatenate([z, w2[1]], axis=1)], axis=0).astype(BF16)


def kernel(x, pre_mix_norm, post_mix_norm, pre_ffn_norm, post_ffn_norm, w_in, diff_lambda_q1, diff_lambda_k1, diff_lambda_q2, diff_lambda_k2, diff_subln, rwkv_mu, rwkv_w0, rwkv_w2, rwkv_a0, rwkv_a2, rwkv_g2, rwkv_k_k, rwkv_k_a, rwkv_r_k, rwkv_gn_w, rwkv_gn_b, w_out, w_up, ffn_conv_w, ffn_conv_b, w_down):
    batch, seq, d = x.shape
    depth = w_in.shape[0]
    assert seq % ROW_TILE == 0 and seq % ATTN_TK == 0 and ROW_TILE % CHUNK == 0
    m = batch * seq
    nc_total = m // CHUNK
    cos_t, sin_t = _rope_tables(seq)
    tri, bd = _scan_constants(ROW_TILE)
    x2 = x.reshape(m, d)
    diff_cols = 3 * DIFF_WIDTH
    row2 = lambda a: a.reshape(1, -1)
    for l in range(depth):
        lambda_init = 0.8 - 0.6 * math.exp(-0.3 * l)
        wqk = w_in[l][:, :2 * DIFF_WIDTH].astype(BF16)
        wvt = w_in[l][:, 2 * DIFF_WIDTH:diff_cols].T.astype(BF16)
        wrw = w_in[l][:, diff_cols:].astype(BF16)
        q0, q1, k, vt, rw = _in_proj(x2, row2(pre_mix_norm[l]), wqk, wvt, wrw, cos_t, sin_t, seq)
        lam_p = jnp.stack([diff_lambda_q1[l], diff_lambda_k1[l], diff_lambda_q2[l], diff_lambda_k2[l]])
        ya = _attention(lam_p, diff_subln[l].reshape(-1, 1), q0, q1, k, vt, batch, seq, lambda_init)
        prep = _rwkv_prep(rw, row2(rwkv_mu[l]), rwkv_w0[l], _lora_blockdiag(rwkv_w2[l]), rwkv_a0[l],
                          _lora_blockdiag(rwkv_a2[l]), rwkv_g2[l].astype(BF16), row2(rwkv_k_k[l]),
                          row2(rwkv_k_a[l]), row2(rwkv_r_k[l]), bd, tri, seq)
        vb, g, bonus = prep[0], prep[1], prep[2]
        dir0 = list(prep[3:9]) + [prep[9].reshape(nc_total, 1, RWKV_WIDTH)]
        dir1 = list(prep[10:16]) + [prep[16].reshape(nc_total, 1, RWKV_WIDTH)]
        yf, yb = _wkv(vb, dir0, dir1, batch, seq)
        x2 = _out_proj(x2, ya, yf, yb, bonus, g, row2(rwkv_gn_w[l]), row2(rwkv_gn_b[l]), bd,
                       w_out[l].astype(BF16), row2(post_mix_norm[l]))
        x2 = _ffn(x2, row2(pre_ffn_norm[l]), w_up[l].astype(BF16), ffn_conv_w[l], row2(ffn_conv_b[l]),
                  w_down[l].astype(BF16), row2(post_ffn_norm[l]), seq)
    return x2.reshape(batch, seq, d)
```

```python
import functools
import math

import jax
import jax.numpy as jnp
from jax import lax
from jax.experimental import pallas as pl
from jax.experimental.pallas import tpu as pltpu

F32 = jnp.float32
BF16 = jnp.bfloat16

LANES = 128
SUBLANES = 8
VMEM_LIMIT = 56 * 1024 * 1024

DIFF_QK_DIM = 64
DIFF_V_DIM = 128
DIFF_HEADS = 4
DIFF_WIDTH = DIFF_HEADS * DIFF_V_DIM
RWKV_HEAD_DIM = 64
RWKV_WIDTH = 512
RWKV_PAIRS = RWKV_WIDTH // LANES
DECAY_LORA = 64
ICLR_LORA = 64
GATE_LORA = 128
ROPE_THETA = 10000.0
NORM_EPS = 1e-6
GN_EPS = 64e-5
CHUNK = 64
WKV_SUB = 2

ROW_TILE = 512
FFN_ROW_TILE = 1024
ATTN_TQ = 256
ATTN_TK = 1024
ATTN_VROWS = 144
FF_CHUNK = 256


def _rms(x, g):
    return x * lax.rsqrt(jnp.mean(x * x, axis=-1, keepdims=True) + NORM_EPS) * g


def _mm(a, b):
    return jnp.dot(a.astype(BF16), b.astype(BF16), preferred_element_type=F32)


def _mm_nt(a, b):
    return lax.dot_general(a.astype(BF16), b.astype(BF16), (((1,), (1,)), ((), ())),
                           preferred_element_type=F32)


def _mm_tn(a, b):
    return lax.dot_general(a.astype(BF16), b.astype(BF16), (((0,), (0,)), ((), ())),
                           preferred_element_type=F32)


def _split2(x):
    hi = x.astype(BF16)
    lo = (x - hi.astype(F32)).astype(BF16)
    return hi, lo


def _seg_sum(x, bd):
    hi, lo = _split2(x)
    tiles = []
    for s in range(x.shape[1] // LANES):
        sl = slice(s * LANES, (s + 1) * LANES)
        tiles.append(jnp.dot(hi[:, sl], bd, preferred_element_type=F32)
                     + jnp.dot(lo[:, sl], bd, preferred_element_type=F32))
    return jnp.concatenate(tiles, axis=1)


def _chunk_cumsum(x, tri):
    hi, lo = _split2(x)
    chunks = []
    for c in range(x.shape[0] // CHUNK):
        rows = slice(c * CHUNK, (c + 1) * CHUNK)
        chunks.append(jnp.dot(tri, hi[rows], preferred_element_type=F32)
                      + jnp.dot(tri, lo[rows], preferred_element_type=F32))
    return jnp.concatenate(chunks, axis=0)


def _params(semantics):
    return pltpu.CompilerParams(dimension_semantics=semantics, vmem_limit_bytes=VMEM_LIMIT)


def _inproj_kernel(x_ref, xp_ref, xn_ref, g_ref, wqk_ref, wvt_ref, wrw_ref, cos_ref, sin_ref, *rest,
                   tiles_per_seq):
    prep_in, (q0_ref, q1_ref, k_ref, vt_ref), prep_out = rest[:11], rest[11:15], rest[15:]
    tm = x_ref.shape[0]
    hb = _rms(x_ref[...], g_ref[...]).astype(BF16)
    qk = jnp.dot(hb, wqk_ref[...], preferred_element_type=F32)
    cos = cos_ref[...]
    sin = sin_ref[...]
    lane = lax.broadcasted_iota(jnp.int32, (tm, LANES), 1)
    first_half = (lane % DIFF_QK_DIM) < (DIFF_QK_DIM // 2)
    comp0 = lane < DIFF_QK_DIM
    scale = DIFF_QK_DIM ** -0.5
    for j in range(2 * DIFF_HEADS):
        t = qk[:, j * LANES:(j + 1) * LANES]
        partner = jnp.where(first_half, pltpu.roll(t, LANES - DIFF_QK_DIM // 2, 1),
                            pltpu.roll(t, DIFF_QK_DIM // 2, 1))
        o = t * cos + partner * sin
        if j < DIFF_HEADS:
            o = o * scale
            q0_ref[:, j * LANES:(j + 1) * LANES] = jnp.where(comp0, o, 0.0).astype(BF16)
            q1_ref[:, j * LANES:(j + 1) * LANES] = jnp.where(comp0, 0.0, o).astype(BF16)
        else:
            jj = j - DIFF_HEADS
            k_ref[:, jj * LANES:(jj + 1) * LANES] = o.astype(BF16)
    vt = lax.dot_general(wvt_ref[...], hb, (((1,), (1,)), ((), ())), preferred_element_type=F32).astype(BF16)
    pad_rows = ATTN_VROWS - DIFF_V_DIM
    ones_row = lax.broadcasted_iota(jnp.int32, (pad_rows, tm), 0) == 0
    pad = jnp.where(ones_row, 1.0, 0.0).astype(BF16)
    for h in range(DIFF_HEADS):
        vt_ref[h * ATTN_VROWS:h * ATTN_VROWS + DIFF_V_DIM, :] = vt[h * DIFF_V_DIM:(h + 1) * DIFF_V_DIM]
        vt_ref[h * ATTN_VROWS + DIFF_V_DIM:(h + 1) * ATTN_VROWS, :] = pad
    z = jnp.dot(hb, wrw_ref[...], preferred_element_type=F32)
    halo = jnp.concatenate([xp_ref[...], xn_ref[...]], axis=0)
    z_halo = jnp.dot(_rms(halo, g_ref[...]).astype(BF16), wrw_ref[...], preferred_element_type=F32)
    i = pl.program_id(0)
    first = (i % tiles_per_seq) == 0
    last = (i % tiles_per_seq) == tiles_per_seq - 1
    prow = jnp.where(first, 0.0, z_halo[SUBLANES - 1:SUBLANES, :])
    nrow = jnp.where(last, 0.0, z_halo[SUBLANES:SUBLANES + 1, :])
    _rwkv_prep_body(z, prow, nrow, *prep_in, *prep_out)


def _in_proj(x2, g, wqk, wvt, wrw, cos_t, sin_t, prep_params, seq):
    m, d = x2.shape
    tm = ROW_TILE
    tps = seq // tm
    w = RWKV_WIDTH
    nblk8 = m // SUBLANES
    row = lambda i: (i, 0)
    const2 = lambda i: (0, 0)
    const3 = lambda i: (0, 0, 0)
    full = lambda a: pl.BlockSpec(a.shape, const2 if a.ndim == 2 else const3, pipeline_mode=pl.Buffered(1))
    big_bf = jax.ShapeDtypeStruct((m, w), BF16)
    big_f = jax.ShapeDtypeStruct((m, w), F32)
    per_dir_shapes = [big_bf] * 6 + [jax.ShapeDtypeStruct((m // CHUNK, w), F32)]
    per_dir_specs = [pl.BlockSpec((tm, w), row)] * 6 + [pl.BlockSpec((tm // CHUNK, w), row)]
    return pl.pallas_call(
        functools.partial(_inproj_kernel, tiles_per_seq=tps),
        grid=(m // tm,),
        in_specs=[
            pl.BlockSpec((tm, d), row),
            pl.BlockSpec((SUBLANES, d), lambda i: (jnp.maximum(i * (tm // SUBLANES) - 1, 0), 0)),
            pl.BlockSpec((SUBLANES, d), lambda i: (jnp.minimum((i + 1) * (tm // SUBLANES), nblk8 - 1), 0)),
            full(g), full(wqk), full(wvt), full(wrw),
            pl.BlockSpec((tm, LANES), lambda i: (i % tps, 0)),
            pl.BlockSpec((tm, LANES), lambda i: (i % tps, 0)),
        ] + [full(a) for a in prep_params],
        out_specs=[
            pl.BlockSpec((tm, DIFF_WIDTH), row),
            pl.BlockSpec((tm, DIFF_WIDTH), row),
            pl.BlockSpec((tm, DIFF_WIDTH), row),
            pl.BlockSpec((DIFF_HEADS * ATTN_VROWS, tm), lambda i: (i // tps, i % tps)),
        ] + [pl.BlockSpec((tm, w), row)] * 3 + per_dir_specs * 2,
        out_shape=[
            jax.ShapeDtypeStruct((m, DIFF_WIDTH), BF16),
            jax.ShapeDtypeStruct((m, DIFF_WIDTH), BF16),
            jax.ShapeDtypeStruct((m, DIFF_WIDTH), BF16),
            jax.ShapeDtypeStruct((m // seq * DIFF_HEADS * ATTN_VROWS, seq), BF16),
        ] + [big_bf, big_f, big_f] + per_dir_shapes * 2,
        compiler_params=_params(("parallel",)),
        name="in_proj",
    )(x2, x2, x2, g, wqk, wvt, wrw, cos_t, sin_t, *prep_params)


def _attn_kernel(lam_ref, sub_ref, q0_ref, q1_ref, q0n_ref, q1n_ref, k_ref, vt_ref, o_ref,
                 s_sc, mx_sc, m_sc, acc_sc, *, tk, lambda_init):
    seq = k_ref.shape[0]
    nk = seq // tk
    qs = (q0_ref[...], q1_ref[...])

    def scores(qpair, i, slot):
        start = i * tk
        ks = k_ref[pl.ds(start, tk), :]
        maxima = []
        for c in range(2):
            st = lax.dot_general(ks, qpair[c], (((1,), (1,)), ((), ())), preferred_element_type=F32)
            s_sc[slot, c] = st
            maxima.append(jnp.max(st, axis=0, keepdims=True))
        return tuple(maxima)

    @pl.when(pl.program_id(2) == 0)
    def _():
        first = scores(qs, 0, 0)
        mx_sc[0] = first[0]
        mx_sc[1] = first[1]

    m_sc[...] = jnp.full(m_sc.shape, -jnp.inf, F32)
    acc_sc[...] = jnp.zeros(acc_sc.shape, F32)

    def accumulate(i, slot, maxima):
        start = i * tk
        vt = vt_ref[:, pl.ds(start, tk)]
        for c in range(2):
            m_old = m_sc[c]
            m_new = jnp.maximum(m_old, maxima[c])
            alpha = jnp.exp(m_old - m_new)
            p = jnp.exp(s_sc[slot, c] - m_new).astype(BF16)
            acc_sc[c] = alpha * acc_sc[c] + jnp.dot(vt, p, preferred_element_type=F32)
            m_sc[c] = m_new

    maxima = (mx_sc[0], mx_sc[1])
    for blk in range(nk):
        if blk + 1 < nk:
            ahead = scores(qs, blk + 1, (blk + 1) % 2)
        else:
            ahead = scores((q0n_ref[...], q1n_ref[...]), 0, 0)
        accumulate(blk, blk % 2, maxima)
        maxima = ahead
    mx_sc[0] = maxima[0]
    mx_sc[1] = maxima[1]

    lp = lam_ref[...]
    lam = (jnp.exp(jnp.sum(lp[0:1] * lp[1:2], axis=-1, keepdims=True))
           - jnp.exp(jnp.sum(lp[2:3] * lp[3:4], axis=-1, keepdims=True)) + lambda_init)
    a0 = acc_sc[0]
    a1 = acc_sc[1]
    dv = DIFF_V_DIM
    o = a0[:dv] / a0[dv:dv + 1] - lam * (a1[:dv] / a1[dv:dv + 1])
    o = o * lax.rsqrt(jnp.mean(o * o, axis=0, keepdims=True) + NORM_EPS) * sub_ref[...] * (1.0 - lambda_init)
    o_ref[...] = o.T.astype(BF16)


def _attention(lam_p, subln_col, q0, q1, k, vt, batch, seq, lambda_init):
    tq, tk = ATTN_TQ, min(ATTN_TK, seq // 2)
    assert (seq // tk) % 2 == 0
    nq = seq // tq
    qmap = lambda b, h, i: (b * nq + i, h)
    qnext = lambda b, h, i: (b * nq + jnp.minimum(i + 1, nq - 1), h)
    const = lambda b, h, i: (0, 0)
    return pl.pallas_call(
        functools.partial(_attn_kernel, tk=tk, lambda_init=lambda_init),
        grid=(batch, DIFF_HEADS, nq),
        in_specs=[
            pl.BlockSpec(lam_p.shape, const),
            pl.BlockSpec(subln_col.shape, const),
            pl.BlockSpec((tq, LANES), qmap),
            pl.BlockSpec((tq, LANES), qmap),
            pl.BlockSpec((tq, LANES), qnext),
            pl.BlockSpec((tq, LANES), qnext),
            pl.BlockSpec((seq, LANES), lambda b, h, i: (b, h)),
            pl.BlockSpec((ATTN_VROWS, seq), lambda b, h, i: (b * DIFF_HEADS + h, 0)),
        ],
        out_specs=pl.BlockSpec((tq, LANES), qmap),
        out_shape=jax.ShapeDtypeStruct((batch * seq, DIFF_WIDTH), BF16),
        scratch_shapes=[
            pltpu.VMEM((2, 2, tk, tq), F32),
            pltpu.VMEM((2, 1, tq), F32),
            pltpu.VMEM((2, 1, tq), F32),
            pltpu.VMEM((2, ATTN_VROWS, tq), F32),
        ],
        compiler_params=_params(("parallel", "parallel", "arbitrary")),
        name="diff_attn",
    )(lam_p, subln_col, q0, q1, q0, q1, k, vt)


def _sigmoid(x):
    return 0.5 * jnp.tanh(0.5 * x) + 0.5


def _rwkv_prep_body(z, prow, nrow, mu_ref, w0_ref, w2_ref, a0_ref, a2_ref, g2_ref,
                    kk_ref, ka_ref, rk_ref, bd_ref, tri_ref,
                    v_o, g_o, bonus_o,
                    at0_o, rt0_o, bb0_o, kb0_o, bh0_o, kh0_o, pl0_o,
                    at1_o, rt1_o, bb1_o, kb1_o, bh1_o, kh1_o, pl1_o):
    tm = z.shape[0]
    w = RWKV_WIDTH
    row = lax.broadcasted_iota(jnp.int32, (tm, 1), 0)
    zp = jnp.where(row == 0, prow, pltpu.roll(z, 1, 0))
    zn = jnp.where(row == tm - 1, nrow, pltpu.roll(z, tm - 1, 0))
    zs = z + (0.5 * (zp + zn) - z) * mu_ref[...]

    r = zs[:, 0:w]
    k = zs[:, w:2 * w]
    v = zs[:, 2 * w:3 * w]
    o3 = 3 * w
    wd = zs[:, o3:o3 + 2 * DECAY_LORA]
    ad = zs[:, o3 + 2 * DECAY_LORA:o3 + 2 * DECAY_LORA + 2 * ICLR_LORA]
    gd = zs[:, o3 + 2 * DECAY_LORA + 2 * ICLR_LORA:]
    bd = bd_ref[...]

    g_o[...] = _mm(_sigmoid(gd), g2_ref[...])
    v_o[...] = v.astype(BF16)
    kkf = k * kk_ref[...]
    kk = kkf * lax.rsqrt(jnp.maximum(_seg_sum(kkf * kkf, bd), 1e-12))
    wl_all = _mm(jnp.tanh(wd), w2_ref[...])
    al_all = _mm(ad, a2_ref[...])
    outs = ((at0_o, rt0_o, bb0_o, kb0_o, bh0_o, kh0_o, pl0_o),
            (at1_o, rt1_o, bb1_o, kb1_o, bh1_o, kh1_o, pl1_o))
    bonus = jnp.zeros((tm, w), F32)
    for d in range(2):
        at_o, rt_o, bb_o, kb_o, bh_o, kh_o, pl_o = outs[d]
        lw = -math.exp(-0.5) * _sigmoid(w0_ref[d:d + 1, :] + wl_all[:, d * w:(d + 1) * w])
        a = _sigmoid(a0_ref[d:d + 1, :] + al_all[:, d * w:(d + 1) * w])
        kd = k * (1.0 + (a - 1.0) * ka_ref[...])
        bonus = bonus + _seg_sum(r * kd * rk_ref[...], bd) * v
        cin = _chunk_cumsum(lw, tri_ref[d])
        cin3 = cin.reshape(tm // CHUNK, CHUNK, w)
        tot3 = cin3[:, CHUNK - 1:CHUNK, :] if d == 0 else cin3[:, 0:1, :]
        rem = (tot3 - cin3).reshape(tm, w)
        tot = tot3.reshape(tm // CHUNK, w)
        e_in = jnp.exp(cin)
        e_neg = jnp.exp(-cin)
        e_rem = jnp.exp(rem)
        beta = kk * a
        at_o[...] = (-kk * jnp.exp(cin - lw)).astype(BF16)
        rt_o[...] = (r * e_in).astype(BF16)
        bb_o[...] = (beta * e_neg).astype(BF16)
        kb_o[...] = (kd * e_neg).astype(BF16)
        bh_o[...] = (beta * e_rem).astype(BF16)
        kh_o[...] = (kd * e_rem).astype(BF16)
        pl_o[...] = jnp.exp(tot)
    bonus_o[...] = bonus


def _wkv_kernel(v0_ref, at0, rt0, bb0, kb0, bh0, kh0, pl0,
                v1_ref, at1, rt1, bb1, kb1, bh1, kh1, pl1,
                y0_ref, y1_ref, s_sc):
    c = pl.program_id(1)

    @pl.when(c == 0)
    def _():
        s_sc[...] = jnp.zeros(s_sc.shape, F32)

    n = 2 * CHUNK
    ri = lax.broadcasted_iota(jnp.int32, (n, n), 0)
    ci = lax.broadcasted_iota(jnp.int32, (n, n), 1)
    ti, tj = ri % CHUNK, ci % CHUNK
    same = (ri // CHUNK) == (ci // CHUNK)
    eye = ri == ci
    head0 = lax.broadcasted_iota(jnp.int32, (CHUNK, LANES), 1) < RWKV_HEAD_DIM

    def stack(x):
        zero = jnp.zeros_like(x)
        return jnp.concatenate([jnp.where(head0, x, zero), jnp.where(head0, zero, x)], axis=0)

    dirs = ((v0_ref, at0, rt0, bb0, kb0, bh0, kh0, pl0, y0_ref),
            (v1_ref, at1, rt1, bb1, kb1, bh1, kh1, pl1, y1_ref))
    units = []
    for d, (v_ref, at_r, rt_r, bb_r, kb_r, bh_r, kh_r, pl_r, y_ref) in enumerate(dirs):
        before = (ti > tj) if d == 0 else (ti < tj)
        strict = same & before
        incl = strict | eye
        levels = []
        b = 1
        while b < CHUNK:
            blk = same & ((ti // (2 * b)) == (tj // (2 * b)))
            hi, hj = (ti // b) % 2, (tj // b) % 2
            levels.append(blk & ((hi == 1) & (hj == 0) if d == 0 else (hi == 0) & (hj == 1)))
            b *= 2
        for p in range(RWKV_PAIRS):
            for j in (range(WKV_SUB) if d == 0 else range(WKV_SUB - 1, -1, -1)):
                units.append((d, p, slice(p * LANES, (p + 1) * LANES), strict, incl, levels,
                              (v_ref, at_r, rt_r, bb_r, kb_r, bh_r, kh_r, pl_r, y_ref),
                              j, slice(j * CHUNK, (j + 1) * CHUNK)))
    nu = range(len(units))

    def stacked(k):
        return [stack(u[6][k][u[8], u[2]]) for u in units]

    v_s, at_s, rt_s, bb_s, kb_s, bh_s, kh_s = (stacked(k) for k in range(7))
    gram = [_mm_nt(jnp.concatenate([at_s[i], rt_s[i]], axis=0), jnp.concatenate([bb_s[i], kb_s[i]], axis=0))
            for i in nu]
    a_ab = [jnp.where(units[i][3], gram[i][:n, :n], 0.0) for i in nu]
    a_ak = [jnp.where(units[i][3], gram[i][:n, n:], 0.0).astype(BF16) for i in nu]
    a_rb = [jnp.where(units[i][4], gram[i][n:, :n], 0.0).astype(BF16) for i in nu]
    a_rk = [jnp.where(units[i][4], gram[i][n:, n:], 0.0).astype(BF16) for i in nu]
    t_inv = [jnp.where(eye, 1.0, jnp.where(units[i][5][0], a_ab[i], 0.0)) for i in nu]
    for lv in range(1, len(units[0][5])):
        t_b = [t.astype(BF16) for t in t_inv]
        half = [_mm(t_b[i], jnp.where(units[i][5][lv], a_ab[i], 0.0)) for i in nu]
        t_inv = [t_inv[i] + _mm(half[i], t_b[i]) for i in nu]
    t_b = [t.astype(BF16) for t in t_inv]
    av = [_mm(jnp.concatenate([a_ak[i], a_rk[i]], axis=0), v_s[i]) for i in nu]
    akv = [av[i][:n].astype(BF16) for i in nu]
    wu = [_mm(t_b[i], jnp.concatenate([at_s[i], akv[i]], axis=1)).astype(BF16) for i in nu]
    qy = [_mm(a_rb[i], wu[i]) for i in nu]
    y0 = [qy[i][:, LANES:] + av[i][n:] for i in nu]
    qt = [rt_s[i].astype(F32) + qy[i][:, :LANES] for i in nu]
    mn = [_mm_tn(wu[i], bh_s[i]) for i in nu]
    n0t = [mn[i][LANES:, :] + _mm_tn(v_s[i], kh_s[i]) for i in nu]
    state = {(d, p): s_sc[d, p] for d in range(2) for p in range(RWKV_PAIRS)}
    for k in range(WKV_SUB):
        for i in nu:
            if i % WKV_SUB != k:
                continue
            d, p, sl, rows = units[i][0], units[i][1], units[i][2], units[i][8]
            pl_r, y_ref = units[i][6][7], units[i][6][8]
            st = state[d, p]
            ys = _mm_nt(qt[i], st) + y0[i]
            y_ref[rows, sl] = ys[:CHUNK] + ys[CHUNK:]
            state[d, p] = st * pl_r[units[i][7], :, sl] + _mm(st, mn[i][:LANES, :]) + n0t[i]
    for (d, p), st in state.items():
        s_sc[d, p] = st


def _wkv(v, dir0, dir1, batch, seq):
    m, w = v.shape
    nc = seq // (CHUNK * WKV_SUB)
    fmap = lambda b, c: (b * nc + c, 0)
    bmap = lambda b, c: (b * nc + nc - 1 - c, 0)
    fmap3 = lambda b, c: (b * nc + c, 0, 0)
    bmap3 = lambda b, c: (b * nc + nc - 1 - c, 0, 0)
    blk = lambda im: pl.BlockSpec((CHUNK * WKV_SUB, w), im)
    in_specs = ([blk(fmap)] * 7 + [pl.BlockSpec((WKV_SUB, 1, w), fmap3)]
                + [blk(bmap)] * 7 + [pl.BlockSpec((WKV_SUB, 1, w), bmap3)])
    return pl.pallas_call(
        _wkv_kernel,
        grid=(batch, nc),
        in_specs=in_specs,
        out_specs=[blk(fmap), blk(bmap)],
        out_shape=[jax.ShapeDtypeStruct((m, w), F32)] * 2,
        scratch_shapes=[pltpu.VMEM((2, RWKV_PAIRS, LANES, LANES), F32)],
        compiler_params=_params(("parallel", "arbitrary")),
        name="wkv_scan",
    )(v, *dir0, v, *dir1)


def _outproj_kernel(x_ref, ya_ref, yf_ref, yb_ref, bonus_ref, g_ref, gnw_ref, gnb_ref, bd_ref,
                    wout_ref, pg_ref, o_ref):
    bd = bd_ref[...]
    inv_n = 1.0 / RWKV_HEAD_DIM
    y = yf_ref[...] + yb_ref[...]
    mean = _seg_sum(y, bd) * inv_n
    yc = y - mean
    var = _seg_sum(yc * yc, bd) * inv_n
    yn = yc * lax.rsqrt(var + GN_EPS) * gnw_ref[...] + gnb_ref[...]
    yb = ((yn + bonus_ref[...]) * g_ref[...]).astype(BF16)
    mix = jnp.dot(jnp.concatenate([ya_ref[...], yb], axis=1), wout_ref[...], preferred_element_type=F32)
    o_ref[...] = x_ref[...] + _rms(mix, pg_ref[...])


def _out_proj(x2, ya, yf, yb, bonus, g, gn_w, gn_b, bd, wout, post_g):
    m, d = x2.shape
    tm = ROW_TILE
    w = RWKV_WIDTH
    row = lambda i: (i, 0)
    const = lambda i: (0, 0)
    full = lambda a: pl.BlockSpec(a.shape, const, pipeline_mode=pl.Buffered(1))
    return pl.pallas_call(
        _outproj_kernel,
        grid=(m // tm,),
        in_specs=[pl.BlockSpec((tm, d), row), pl.BlockSpec((tm, DIFF_WIDTH), row)]
                 + [pl.BlockSpec((tm, w), row)] * 4
                 + [full(gn_w), full(gn_b), full(bd), full(wout), full(post_g)],
        out_specs=pl.BlockSpec((tm, d), row),
        out_shape=jax.ShapeDtypeStruct((m, d), F32),
        compiler_params=_params(("parallel",)),
        name="out_proj",
    )(x2, ya, yf, yb, bonus, g, gn_w, gn_b, bd, wout, post_g)


def _gelu_tanh(x):
    c = math.sqrt(2.0 / math.pi)
    return 0.5 * x * (1.0 + jnp.tanh(c * (x + 0.044715 * (x * x * x))))


def _ffn_kernel(x_ref, xp_ref, xn_ref, g_ref, wup_ref, cw_ref, cb_ref, wdn_ref, pg_ref, o_ref, act_sc,
                *, tiles_per_seq, d_ff):
    tm = x_ref.shape[0]
    i = pl.program_id(0)
    first = (i % tiles_per_seq) == 0
    last = (i % tiles_per_seq) == tiles_per_seq - 1
    g = g_ref[...]
    x = x_ref[...]
    hb = _rms(x, g).astype(BF16)
    hp = _rms(xp_ref[...], g).astype(BF16)
    hn = _rms(xn_ref[...], g).astype(BF16)
    hext = jnp.concatenate([hp, hb, hn], axis=0)
    ext = tm + 2 * SUBLANES
    row = lax.broadcasted_iota(jnp.int32, (ext, 1), 0)
    kill_prev = jnp.logical_and(first, row == SUBLANES)
    kill_next = jnp.logical_and(last, row == SUBLANES + tm - 1)
    for j in range(d_ff // FF_CHUNK):
        cs = slice(j * FF_CHUNK, (j + 1) * FF_CHUNK)
        gate = jnp.dot(hext, wup_ref[:, cs], preferred_element_type=F32)
        up = jnp.dot(hb, wup_ref[:, d_ff + j * FF_CHUNK:d_ff + (j + 1) * FF_CHUNK],
                     preferred_element_type=F32)
        gp = jnp.where(kill_prev, 0.0, pltpu.roll(gate, 1, 0))
        gn = jnp.where(kill_next, 0.0, pltpu.roll(gate, ext - 1, 0))
        conv = cw_ref[0:1, cs] * gp + cw_ref[1:2, cs] * gate + cw_ref[2:3, cs] * gn + cb_ref[:, cs]
        conv = conv[SUBLANES:SUBLANES + tm, :]
        act_sc[:, cs] = (_gelu_tanh(conv) * up).astype(BF16)
    down = jnp.dot(act_sc[...], wdn_ref[...], preferred_element_type=F32)
    o_ref[...] = x + _rms(down, pg_ref[...])


def _ffn(x1, pre_g, wup, conv_w, conv_b, wdn, post_g, seq):
    m, d = x1.shape
    tm = min(FFN_ROW_TILE, seq)
    d_ff = wdn.shape[0]
    nblk8 = m // SUBLANES
    row = lambda i: (i, 0)
    const = lambda i: (0, 0)
    full = lambda a: pl.BlockSpec(a.shape, const, pipeline_mode=pl.Buffered(1))
    return pl.pallas_call(
        functools.partial(_ffn_kernel, tiles_per_seq=seq // tm, d_ff=d_ff),
        grid=(m // tm,),
        in_specs=[
            pl.BlockSpec((tm, d), row),
            pl.BlockSpec((SUBLANES, d), lambda i: (jnp.maximum(i * (tm // SUBLANES) - 1, 0), 0)),
            pl.BlockSpec((SUBLANES, d), lambda i: (jnp.minimum((i + 1) * (tm // SUBLANES), nblk8 - 1), 0)),
            full(pre_g), full(wup), full(conv_w), full(conv_b), full(wdn), full(post_g),
        ],
        out_specs=pl.BlockSpec((tm, d), row),
        out_shape=jax.ShapeDtypeStruct((m, d), F32),
        scratch_shapes=[pltpu.VMEM((tm, d_ff), BF16)],
        compiler_params=_params(("parallel",)),
        name="conv_ffn",
    )(x1, x1, x1, pre_g, wup, conv_w, conv_b, wdn, post_g)


def _rope_tables(seq):
    half = DIFF_QK_DIM // 2
    pos = jnp.arange(seq, dtype=F32)
    inv = ROPE_THETA ** (-jnp.arange(half, dtype=F32) / half)
    ang = pos[:, None] * inv[None, :]
    cos, sin = jnp.cos(ang), jnp.sin(ang)
    reps = LANES // DIFF_QK_DIM
    cos_t = jnp.tile(jnp.concatenate([cos, cos], axis=-1), (1, reps))
    sin_t = jnp.tile(jnp.concatenate([-sin, sin], axis=-1), (1, reps))
    return cos_t, sin_t


def _scan_constants():
    t = jnp.arange(CHUNK)
    tri = jnp.stack([t[:, None] >= t[None, :],
                     t[:, None] <= t[None, :]])
    ch = jnp.arange(LANES)
    bd = (ch[:, None] // RWKV_HEAD_DIM) == (ch[None, :] // RWKV_HEAD_DIM)
    return tri.astype(BF16), bd.astype(BF16)


def _lora_blockdiag(w2):
    z = jnp.zeros_like(w2[0])
    return jnp.concatenate([jnp.concatenate([w2[0], z], axis=1),
                            jnp.concatenate([z, w2[1]], axis=1)], axis=0).astype(BF16)


def kernel(x, pre_mix_norm, post_mix_norm, pre_ffn_norm, post_ffn_norm, w_in, diff_lambda_q1, diff_lambda_k1, diff_lambda_q2, diff_lambda_k2, diff_subln, rwkv_mu, rwkv_w0, rwkv_w2, rwkv_a0, rwkv_a2, rwkv_g2, rwkv_k_k, rwkv_k_a, rwkv_r_k, rwkv_gn_w, rwkv_gn_b, w_out, w_up, ffn_conv_w, ffn_conv_b, w_down):
    batch, seq, d = x.shape
    depth = w_in.shape[0]
    assert seq % ROW_TILE == 0 and seq % ATTN_TK == 0 and ROW_TILE % CHUNK == 0
    m = batch * seq
    nc_total = m // CHUNK
    cos_t, sin_t = _rope_tables(seq)
    tri, bd = _scan_constants()
    x2 = x.reshape(m, d)
    diff_cols = 3 * DIFF_WIDTH
    row2 = lambda a: a.reshape(1, -1)
    for l in range(depth):
        lambda_init = 0.8 - 0.6 * math.exp(-0.3 * l)
        wqk = w_in[l][:, :2 * DIFF_WIDTH].astype(BF16)
        wvt = w_in[l][:, 2 * DIFF_WIDTH:diff_cols].T.astype(BF16)
        wrw = w_in[l][:, diff_cols:].astype(BF16)
        prep_params = (row2(rwkv_mu[l]), rwkv_w0[l], _lora_blockdiag(rwkv_w2[l]), rwkv_a0[l],
                       _lora_blockdiag(rwkv_a2[l]), rwkv_g2[l].astype(BF16), row2(rwkv_k_k[l]),
                       row2(rwkv_k_a[l]), row2(rwkv_r_k[l]), bd, tri)
        q0, q1, k, vt, *prep = _in_proj(x2, row2(pre_mix_norm[l]), wqk, wvt, wrw, cos_t, sin_t, prep_params, seq)
        lam_p = jnp.stack([diff_lambda_q1[l], diff_lambda_k1[l], diff_lambda_q2[l], diff_lambda_k2[l]])
        ya = _attention(lam_p, diff_subln[l].reshape(-1, 1), q0, q1, k, vt, batch, seq, lambda_init)
        vb, g, bonus = prep[0], prep[1], prep[2]
        dir0 = list(prep[3:9]) + [prep[9].reshape(nc_total, 1, RWKV_WIDTH)]
        dir1 = list(prep[10:16]) + [prep[16].reshape(nc_total, 1, RWKV_WIDTH)]
        yf, yb = _wkv(vb, dir0, dir1, batch, seq)
        x2 = _out_proj(x2, ya, yf, yb, bonus, g, row2(rwkv_gn_w[l]), row2(rwkv_gn_b[l]), bd,
                       w_out[l].astype(BF16), row2(post_mix_norm[l]))
        x2 = _ffn(x2, row2(pre_ffn_norm[l]), w_up[l].astype(BF16), ffn_conv_w[l], row2(ffn_conv_b[l]),
                  w_down[l].astype(BF16), row2(post_ffn_norm[l]), seq)
    return x2.reshape(batch, seq, d)
```

```python
import functools
import math

import jax
import jax.numpy as jnp
from jax import lax
from jax.experimental import pallas as pl
from jax.experimental.pallas import tpu as pltpu

F32 = jnp.float32
BF16 = jnp.bfloat16

LANES = 128
SUBLANES = 8
VMEM_LIMIT = 56 * 1024 * 1024

DIFF_QK_DIM = 64
DIFF_V_DIM = 128
DIFF_HEADS = 4
DIFF_WIDTH = DIFF_HEADS * DIFF_V_DIM
RWKV_HEAD_DIM = 64
RWKV_WIDTH = 512
RWKV_PAIRS = RWKV_WIDTH // LANES
DECAY_LORA = 64
ICLR_LORA = 64
GATE_LORA = 128
ROPE_THETA = 10000.0
NORM_EPS = 1e-6
GN_EPS = 64e-5
CHUNK = 64
WKV_SUB = 2

ROW_TILE = 512
FFN_ROW_TILE = 1024
ATTN_TQ = 512
ATTN_TK = 512
ATTN_VROWS = 144
FF_CHUNK = 256


def _rms(x, g):
    return x * lax.rsqrt(jnp.mean(x * x, axis=-1, keepdims=True) + NORM_EPS) * g


def _mm(a, b):
    return jnp.dot(a.astype(BF16), b.astype(BF16), preferred_element_type=F32)


def _mm_nt(a, b):
    return lax.dot_general(a.astype(BF16), b.astype(BF16), (((1,), (1,)), ((), ())),
                           preferred_element_type=F32)


def _mm_tn(a, b):
    return lax.dot_general(a.astype(BF16), b.astype(BF16), (((0,), (0,)), ((), ())),
                           preferred_element_type=F32)


def _split2(x):
    hi = x.astype(BF16)
    lo = (x - hi.astype(F32)).astype(BF16)
    return hi, lo


def _seg_sum(x, bd):
    hi, lo = _split2(x)
    tiles = []
    for s in range(x.shape[1] // LANES):
        sl = slice(s * LANES, (s + 1) * LANES)
        tiles.append(jnp.dot(hi[:, sl], bd, preferred_element_type=F32)
                     + jnp.dot(lo[:, sl], bd, preferred_element_type=F32))
    return jnp.concatenate(tiles, axis=1)


def _chunk_cumsum(x, tri):
    hi, lo = _split2(x)
    chunks = []
    for c in range(x.shape[0] // CHUNK):
        rows = slice(c * CHUNK, (c + 1) * CHUNK)
        chunks.append(jnp.dot(tri, hi[rows], preferred_element_type=F32)
                      + jnp.dot(tri, lo[rows], preferred_element_type=F32))
    return jnp.concatenate(chunks, axis=0)


def _params(semantics):
    return pltpu.CompilerParams(dimension_semantics=semantics, vmem_limit_bytes=VMEM_LIMIT)


def _inproj_kernel(x_ref, xp_ref, xn_ref, g_ref, wqk_ref, wvt_ref, wrw_ref, cos_ref, sin_ref, *rest,
                   tiles_per_seq):
    prep_in, (q0_ref, q1_ref, k_ref, vt_ref), prep_out = rest[:11], rest[11:15], rest[15:]
    tm = x_ref.shape[0]
    hb = _rms(x_ref[...], g_ref[...]).astype(BF16)
    qk = jnp.dot(hb, wqk_ref[...], preferred_element_type=F32)
    cos = cos_ref[...]
    sin = sin_ref[...]
    lane = lax.broadcasted_iota(jnp.int32, (tm, LANES), 1)
    first_half = (lane % DIFF_QK_DIM) < (DIFF_QK_DIM // 2)
    comp0 = lane < DIFF_QK_DIM
    scale = DIFF_QK_DIM ** -0.5
    for j in range(2 * DIFF_HEADS):
        t = qk[:, j * LANES:(j + 1) * LANES]
        partner = jnp.where(first_half, pltpu.roll(t, LANES - DIFF_QK_DIM // 2, 1),
                            pltpu.roll(t, DIFF_QK_DIM // 2, 1))
        o = t * cos + partner * sin
        if j < DIFF_HEADS:
            o = o * scale
            q0_ref[:, j * LANES:(j + 1) * LANES] = jnp.where(comp0, o, 0.0).astype(BF16)
            q1_ref[:, j * LANES:(j + 1) * LANES] = jnp.where(comp0, 0.0, o).astype(BF16)
        else:
            jj = j - DIFF_HEADS
            k_ref[:, jj * LANES:(jj + 1) * LANES] = o.astype(BF16)
    vt = lax.dot_general(wvt_ref[...], hb, (((1,), (1,)), ((), ())), preferred_element_type=F32).astype(BF16)
    pad_rows = ATTN_VROWS - DIFF_V_DIM
    ones_row = lax.broadcasted_iota(jnp.int32, (pad_rows, tm), 0) == 0
    pad = jnp.where(ones_row, 1.0, 0.0).astype(BF16)
    for h in range(DIFF_HEADS):
        vt_ref[h * ATTN_VROWS:h * ATTN_VROWS + DIFF_V_DIM, :] = vt[h * DIFF_V_DIM:(h + 1) * DIFF_V_DIM]
        vt_ref[h * ATTN_VROWS + DIFF_V_DIM:(h + 1) * ATTN_VROWS, :] = pad
    z = jnp.dot(hb, wrw_ref[...], preferred_element_type=F32)
    halo = jnp.concatenate([xp_ref[...], xn_ref[...]], axis=0)
    z_halo = jnp.dot(_rms(halo, g_ref[...]).astype(BF16), wrw_ref[...], preferred_element_type=F32)
    i = pl.program_id(0)
    first = (i % tiles_per_seq) == 0
    last = (i % tiles_per_seq) == tiles_per_seq - 1
    prow = jnp.where(first, 0.0, z_halo[SUBLANES - 1:SUBLANES, :])
    nrow = jnp.where(last, 0.0, z_halo[SUBLANES:SUBLANES + 1, :])
    _rwkv_prep_body(z, prow, nrow, *prep_in, *prep_out)


def _in_proj(x2, g, wqk, wvt, wrw, cos_t, sin_t, prep_params, seq):
    m, d = x2.shape
    tm = ROW_TILE
    tps = seq // tm
    w = RWKV_WIDTH
    nblk8 = m // SUBLANES
    row = lambda i: (i, 0)
    const2 = lambda i: (0, 0)
    const3 = lambda i: (0, 0, 0)
    full = lambda a: pl.BlockSpec(a.shape, const2 if a.ndim == 2 else const3, pipeline_mode=pl.Buffered(1))
    big_bf = jax.ShapeDtypeStruct((m, w), BF16)
    big_f = jax.ShapeDtypeStruct((m, w), F32)
    per_dir_shapes = [big_bf] * 6 + [jax.ShapeDtypeStruct((m // CHUNK, w), F32)]
    per_dir_specs = [pl.BlockSpec((tm, w), row)] * 6 + [pl.BlockSpec((tm // CHUNK, w), row)]
    return pl.pallas_call(
        functools.partial(_inproj_kernel, tiles_per_seq=tps),
        grid=(m // tm,),
        in_specs=[
            pl.BlockSpec((tm, d), row),
            pl.BlockSpec((SUBLANES, d), lambda i: (jnp.maximum(i * (tm // SUBLANES) - 1, 0), 0)),
            pl.BlockSpec((SUBLANES, d), lambda i: (jnp.minimum((i + 1) * (tm // SUBLANES), nblk8 - 1), 0)),
            full(g), full(wqk), full(wvt), full(wrw),
            pl.BlockSpec((tm, LANES), lambda i: (i % tps, 0)),
            pl.BlockSpec((tm, LANES), lambda i: (i % tps, 0)),
        ] + [full(a) for a in prep_params],
        out_specs=[
            pl.BlockSpec((tm, DIFF_WIDTH), row),
            pl.BlockSpec((tm, DIFF_WIDTH), row),
            pl.BlockSpec((tm, DIFF_WIDTH), row),
            pl.BlockSpec((DIFF_HEADS * ATTN_VROWS, tm), lambda i: (i // tps, i % tps)),
        ] + [pl.BlockSpec((tm, w), row)] * 3 + per_dir_specs * 2,
        out_shape=[
            jax.ShapeDtypeStruct((m, DIFF_WIDTH), BF16),
            jax.ShapeDtypeStruct((m, DIFF_WIDTH), BF16),
            jax.ShapeDtypeStruct((m, DIFF_WIDTH), BF16),
            jax.ShapeDtypeStruct((m // seq * DIFF_HEADS * ATTN_VROWS, seq), BF16),
        ] + [big_bf, big_f, big_f] + per_dir_shapes * 2,
        compiler_params=_params(("parallel",)),
        name="in_proj",
    )(x2, x2, x2, g, wqk, wvt, wrw, cos_t, sin_t, *prep_params)


def _attn_kernel(lam_ref, sub_ref, q0_ref, q1_ref, q0n_ref, q1n_ref, k_ref, vt_ref, o_ref,
                 s_sc, mx_sc, m_sc, acc_sc, *, tk, lambda_init):
    seq = k_ref.shape[0]
    nk = seq // tk
    qs = (q0_ref[...], q1_ref[...])

    def scores(qpair, i, slot):
        start = i * tk
        ks = k_ref[pl.ds(start, tk), :]
        maxima = []
        for c in range(2):
            st = lax.dot_general(ks, qpair[c], (((1,), (1,)), ((), ())), preferred_element_type=F32)
            s_sc[slot, c] = st
            maxima.append(jnp.max(st, axis=0, keepdims=True))
        return tuple(maxima)

    @pl.when(pl.program_id(2) == 0)
    def _():
        first = scores(qs, 0, 0)
        mx_sc[0] = first[0]
        mx_sc[1] = first[1]

    m_sc[...] = jnp.full(m_sc.shape, -jnp.inf, F32)
    acc_sc[...] = jnp.zeros(acc_sc.shape, F32)

    def accumulate(i, slot, maxima):
        start = i * tk
        vt = vt_ref[:, pl.ds(start, tk)]
        for c in range(2):
            m_old = m_sc[c]
            m_new = jnp.maximum(m_old, maxima[c])
            alpha = jnp.exp(m_old - m_new)
            p = jnp.exp(s_sc[slot, c] - m_new).astype(BF16)
            acc_sc[c] = alpha * acc_sc[c] + jnp.dot(vt, p, preferred_element_type=F32)
            m_sc[c] = m_new

    maxima = (mx_sc[0], mx_sc[1])
    for blk in range(nk):
        if blk + 1 < nk:
            ahead = scores(qs, blk + 1, (blk + 1) % 2)
        else:
            ahead = scores((q0n_ref[...], q1n_ref[...]), 0, 0)
        accumulate(blk, blk % 2, maxima)
        maxima = ahead
    mx_sc[0] = maxima[0]
    mx_sc[1] = maxima[1]

    lp = lam_ref[...]
    lam = (jnp.exp(jnp.sum(lp[0:1] * lp[1:2], axis=-1, keepdims=True))
           - jnp.exp(jnp.sum(lp[2:3] * lp[3:4], axis=-1, keepdims=True)) + lambda_init)
    a0 = acc_sc[0]
    a1 = acc_sc[1]
    dv = DIFF_V_DIM
    o = a0[:dv] / a0[dv:dv + 1] - lam * (a1[:dv] / a1[dv:dv + 1])
    o = o * lax.rsqrt(jnp.mean(o * o, axis=0, keepdims=True) + NORM_EPS) * sub_ref[...] * (1.0 - lambda_init)
    o_ref[...] = o.T.astype(BF16)


def _attention(lam_p, subln_col, q0, q1, k, vt, batch, seq, lambda_init):
    tq, tk = ATTN_TQ, min(ATTN_TK, seq // 2)
    assert (seq // tk) % 2 == 0
    nq = seq // tq
    qmap = lambda b, h, i: (b * nq + i, h)
    qnext = lambda b, h, i: (b * nq + jnp.minimum(i + 1, nq - 1), h)
    const = lambda b, h, i: (0, 0)
    return pl.pallas_call(
        functools.partial(_attn_kernel, tk=tk, lambda_init=lambda_init),
        grid=(batch, DIFF_HEADS, nq),
        in_specs=[
            pl.BlockSpec(lam_p.shape, const),
            pl.BlockSpec(subln_col.shape, const),
            pl.BlockSpec((tq, LANES), qmap),
            pl.BlockSpec((tq, LANES), qmap),
            pl.BlockSpec((tq, LANES), qnext),
            pl.BlockSpec((tq, LANES), qnext),
            pl.BlockSpec((seq, LANES), lambda b, h, i: (b, h)),
            pl.BlockSpec((ATTN_VROWS, seq), lambda b, h, i: (b * DIFF_HEADS + h, 0)),
        ],
        out_specs=pl.BlockSpec((tq, LANES), qmap),
        out_shape=jax.ShapeDtypeStruct((batch * seq, DIFF_WIDTH), BF16),
        scratch_shapes=[
            pltpu.VMEM((2, 2, tk, tq), F32),
            pltpu.VMEM((2, 1, tq), F32),
            pltpu.VMEM((2, 1, tq), F32),
            pltpu.VMEM((2, ATTN_VROWS, tq), F32),
        ],
        compiler_params=_params(("parallel", "parallel", "arbitrary")),
        name="diff_attn",
    )(lam_p, subln_col, q0, q1, q0, q1, k, vt)


def _sigmoid(x):
    return 0.5 * jnp.tanh(0.5 * x) + 0.5


def _rwkv_prep_body(z, prow, nrow, mu_ref, w0_ref, w2_ref, a0_ref, a2_ref, g2_ref,
                    kk_ref, ka_ref, rk_ref, bd_ref, tri_ref,
                    v_o, g_o, bonus_o,
                    at0_o, rt0_o, bb0_o, kb0_o, bh0_o, kh0_o, pl0_o,
                    at1_o, rt1_o, bb1_o, kb1_o, bh1_o, kh1_o, pl1_o):
    tm = z.shape[0]
    w = RWKV_WIDTH
    row = lax.broadcasted_iota(jnp.int32, (tm, 1), 0)
    zp = jnp.where(row == 0, prow, pltpu.roll(z, 1, 0))
    zn = jnp.where(row == tm - 1, nrow, pltpu.roll(z, tm - 1, 0))
    zs = z + (0.5 * (zp + zn) - z) * mu_ref[...]

    r = zs[:, 0:w]
    k = zs[:, w:2 * w]
    v = zs[:, 2 * w:3 * w]
    o3 = 3 * w
    wd = zs[:, o3:o3 + 2 * DECAY_LORA]
    ad = zs[:, o3 + 2 * DECAY_LORA:o3 + 2 * DECAY_LORA + 2 * ICLR_LORA]
    gd = zs[:, o3 + 2 * DECAY_LORA + 2 * ICLR_LORA:]
    bd = bd_ref[...]

    g_o[...] = _mm(_sigmoid(gd), g2_ref[...])
    v_o[...] = v.astype(BF16)
    kkf = k * kk_ref[...]
    kk = kkf * lax.rsqrt(jnp.maximum(_seg_sum(kkf * kkf, bd), 1e-12))
    wl_all = _mm(jnp.tanh(wd), w2_ref[...])
    al_all = _mm(ad, a2_ref[...])
    outs = ((at0_o, rt0_o, bb0_o, kb0_o, bh0_o, kh0_o, pl0_o),
            (at1_o, rt1_o, bb1_o, kb1_o, bh1_o, kh1_o, pl1_o))
    bonus = jnp.zeros((tm, w), F32)
    for d in range(2):
        at_o, rt_o, bb_o, kb_o, bh_o, kh_o, pl_o = outs[d]
        lw = -math.exp(-0.5) * _sigmoid(w0_ref[d:d + 1, :] + wl_all[:, d * w:(d + 1) * w])
        a = _sigmoid(a0_ref[d:d + 1, :] + al_all[:, d * w:(d + 1) * w])
        kd = k * (1.0 + (a - 1.0) * ka_ref[...])
        bonus = bonus + _seg_sum(r * kd * rk_ref[...], bd) * v
        cin = _chunk_cumsum(lw, tri_ref[d])
        cin3 = cin.reshape(tm // CHUNK, CHUNK, w)
        tot3 = cin3[:, CHUNK - 1:CHUNK, :] if d == 0 else cin3[:, 0:1, :]
        rem = (tot3 - cin3).reshape(tm, w)
        tot = tot3.reshape(tm // CHUNK, w)
        e_in = jnp.exp(cin)
        e_neg = jnp.exp(-cin)
        e_rem = jnp.exp(rem)
        beta = kk * a
        at_o[...] = (-kk * jnp.exp(cin - lw)).astype(BF16)
        rt_o[...] = (r * e_in).astype(BF16)
        bb_o[...] = (beta * e_neg).astype(BF16)
        kb_o[...] = (kd * e_neg).astype(BF16)
        bh_o[...] = (beta * e_rem).astype(BF16)
        kh_o[...] = (kd * e_rem).astype(BF16)
        pl_o[...] = jnp.exp(tot)
    bonus_o[...] = bonus


def _wkv_kernel(v0_ref, at0, rt0, bb0, kb0, bh0, kh0, pl0,
                v1_ref, at1, rt1, bb1, kb1, bh1, kh1, pl1,
                y0_ref, y1_ref, s_sc):
    c = pl.program_id(1)

    @pl.when(c == 0)
    def _():
        s_sc[...] = jnp.zeros(s_sc.shape, F32)

    n = 2 * CHUNK
    ri = lax.broadcasted_iota(jnp.int32, (n, n), 0)
    ci = lax.broadcasted_iota(jnp.int32, (n, n), 1)
    ti, tj = ri % CHUNK, ci % CHUNK
    same = (ri // CHUNK) == (ci // CHUNK)
    eye = ri == ci
    head0 = lax.broadcasted_iota(jnp.int32, (CHUNK, LANES), 1) < RWKV_HEAD_DIM

    def stack(x):
        zero = jnp.zeros_like(x)
        return jnp.concatenate([jnp.where(head0, x, zero), jnp.where(head0, zero, x)], axis=0)

    dirs = ((v0_ref, at0, rt0, bb0, kb0, bh0, kh0, pl0, y0_ref),
            (v1_ref, at1, rt1, bb1, kb1, bh1, kh1, pl1, y1_ref))
    units = []
    for d, (v_ref, at_r, rt_r, bb_r, kb_r, bh_r, kh_r, pl_r, y_ref) in enumerate(dirs):
        before = (ti > tj) if d == 0 else (ti < tj)
        strict = same & before
        incl = strict | eye
        levels = []
        b = 1
        while b < CHUNK:
            blk = same & ((ti // (2 * b)) == (tj // (2 * b)))
            hi, hj = (ti // b) % 2, (tj // b) % 2
            levels.append(blk & ((hi == 1) & (hj == 0) if d == 0 else (hi == 0) & (hj == 1)))
            b *= 2
        for p in range(RWKV_PAIRS):
            for j in (range(WKV_SUB) if d == 0 else range(WKV_SUB - 1, -1, -1)):
                units.append((d, p, slice(p * LANES, (p + 1) * LANES), strict, incl, levels,
                              (v_ref, at_r, rt_r, bb_r, kb_r, bh_r, kh_r, pl_r, y_ref),
                              j, slice(j * CHUNK, (j + 1) * CHUNK)))
    nu = range(len(units))

    def stacked(k):
        return [stack(u[6][k][u[8], u[2]]) for u in units]

    v_s, at_s, rt_s, bb_s, kb_s, bh_s, kh_s = (stacked(k) for k in range(7))
    gram = [_mm_nt(jnp.concatenate([at_s[i], rt_s[i]], axis=0), jnp.concatenate([bb_s[i], kb_s[i]], axis=0))
            for i in nu]
    a_ab = [jnp.where(units[i][3], gram[i][:n, :n], 0.0) for i in nu]
    a_ak = [jnp.where(units[i][3], gram[i][:n, n:], 0.0).astype(BF16) for i in nu]
    a_rb = [jnp.where(units[i][4], gram[i][n:, :n], 0.0).astype(BF16) for i in nu]
    a_rk = [jnp.where(units[i][4], gram[i][n:, n:], 0.0).astype(BF16) for i in nu]
    t_inv = [jnp.where(eye, 1.0, jnp.where(units[i][5][0], a_ab[i], 0.0)) for i in nu]
    for lv in range(1, len(units[0][5])):
        t_b = [t.astype(BF16) for t in t_inv]
        half = [_mm(t_b[i], jnp.where(units[i][5][lv], a_ab[i], 0.0)) for i in nu]
        t_inv = [t_inv[i] + _mm(half[i], t_b[i]) for i in nu]
    t_b = [t.astype(BF16) for t in t_inv]
    av = [_mm(jnp.concatenate([a_ak[i], a_rk[i]], axis=0), v_s[i]) for i in nu]
    akv = [av[i][:n].astype(BF16) for i in nu]
    wu = [_mm(t_b[i], jnp.concatenate([at_s[i], akv[i]], axis=1)).astype(BF16) for i in nu]
    qy = [_mm(a_rb[i], wu[i]) for i in nu]
    y0 = [qy[i][:, LANES:] + av[i][n:] for i in nu]
    qt = [rt_s[i].astype(F32) + qy[i][:, :LANES] for i in nu]
    mn = [_mm_tn(wu[i], bh_s[i]) for i in nu]
    n0t = [mn[i][LANES:, :] + _mm_tn(v_s[i], kh_s[i]) for i in nu]
    state = {(d, p): s_sc[d, p] for d in range(2) for p in range(RWKV_PAIRS)}
    for k in range(WKV_SUB):
        for i in nu:
            if i % WKV_SUB != k:
                continue
            d, p, sl, rows = units[i][0], units[i][1], units[i][2], units[i][8]
            pl_r, y_ref = units[i][6][7], units[i][6][8]
            st = state[d, p]
            ys = _mm_nt(qt[i], st) + y0[i]
            y_ref[rows, sl] = ys[:CHUNK] + ys[CHUNK:]
            state[d, p] = st * pl_r[units[i][7], :, sl] + _mm(st, mn[i][:LANES, :]) + n0t[i]
    for (d, p), st in state.items():
        s_sc[d, p] = st


def _wkv(v, dir0, dir1, batch, seq):
    m, w = v.shape
    nc = seq // (CHUNK * WKV_SUB)
    fmap = lambda b, c: (b * nc + c, 0)
    bmap = lambda b, c: (b * nc + nc - 1 - c, 0)
    fmap3 = lambda b, c: (b * nc + c, 0, 0)
    bmap3 = lambda b, c: (b * nc + nc - 1 - c, 0, 0)
    blk = lambda im: pl.BlockSpec((CHUNK * WKV_SUB, w), im)
    in_specs = ([blk(fmap)] * 7 + [pl.BlockSpec((WKV_SUB, 1, w), fmap3)]
                + [blk(bmap)] * 7 + [pl.BlockSpec((WKV_SUB, 1, w), bmap3)])
    return pl.pallas_call(
        _wkv_kernel,
        grid=(batch, nc),
        in_specs=in_specs,
        out_specs=[blk(fmap), blk(bmap)],
        out_shape=[jax.ShapeDtypeStruct((m, w), F32)] * 2,
        scratch_shapes=[pltpu.VMEM((2, RWKV_PAIRS, LANES, LANES), F32)],
        compiler_params=_params(("parallel", "arbitrary")),
        name="wkv_scan",
    )(v, *dir0, v, *dir1)


def _outproj_kernel(x_ref, ya_ref, yf_ref, yb_ref, bonus_ref, g_ref, gnw_ref, gnb_ref, bd_ref,
                    wout_ref, pg_ref, o_ref):
    bd = bd_ref[...]
    inv_n = 1.0 / RWKV_HEAD_DIM
    y = yf_ref[...] + yb_ref[...]
    mean = _seg_sum(y, bd) * inv_n
    yc = y - mean
    var = _seg_sum(yc * yc, bd) * inv_n
    yn = yc * lax.rsqrt(var + GN_EPS) * gnw_ref[...] + gnb_ref[...]
    yb = ((yn + bonus_ref[...]) * g_ref[...]).astype(BF16)
    mix = jnp.dot(jnp.concatenate([ya_ref[...], yb], axis=1), wout_ref[...], preferred_element_type=F32)
    o_ref[...] = x_ref[...] + _rms(mix, pg_ref[...])


def _out_proj(x2, ya, yf, yb, bonus, g, gn_w, gn_b, bd, wout, post_g):
    m, d = x2.shape
    tm = ROW_TILE
    w = RWKV_WIDTH
    row = lambda i: (i, 0)
    const = lambda i: (0, 0)
    full = lambda a: pl.BlockSpec(a.shape, const, pipeline_mode=pl.Buffered(1))
    return pl.pallas_call(
        _outproj_kernel,
        grid=(m // tm,),
        in_specs=[pl.BlockSpec((tm, d), row), pl.BlockSpec((tm, DIFF_WIDTH), row)]
                 + [pl.BlockSpec((tm, w), row)] * 4
                 + [full(gn_w), full(gn_b), full(bd), full(wout), full(post_g)],
        out_specs=pl.BlockSpec((tm, d), row),
        out_shape=jax.ShapeDtypeStruct((m, d), F32),
        compiler_params=_params(("parallel",)),
        name="out_proj",
    )(x2, ya, yf, yb, bonus, g, gn_w, gn_b, bd, wout, post_g)


def _gelu_tanh(x):
    c = math.sqrt(2.0 / math.pi)
    return 0.5 * x * (1.0 + jnp.tanh(c * (x + 0.044715 * (x * x * x))))


def _ffn_kernel(x_ref, xp_ref, xn_ref, g_ref, wup_ref, cw_ref, cb_ref, wdn_ref, pg_ref, o_ref, act_sc,
                *, tiles_per_seq, d_ff):
    tm = x_ref.shape[0]
    i = pl.program_id(0)
    first = (i % tiles_per_seq) == 0
    last = (i % tiles_per_seq) == tiles_per_seq - 1
    g = g_ref[...]
    x = x_ref[...]
    hb = _rms(x, g).astype(BF16)
    hp = _rms(xp_ref[...], g).astype(BF16)
    hn = _rms(xn_ref[...], g).astype(BF16)
    hext = jnp.concatenate([hp, hb, hn], axis=0)
    ext = tm + 2 * SUBLANES
    row = lax.broadcasted_iota(jnp.int32, (ext, 1), 0)
    kill_prev = jnp.logical_and(first, row == SUBLANES)
    kill_next = jnp.logical_and(last, row == SUBLANES + tm - 1)
    for j in range(d_ff // FF_CHUNK):
        cs = slice(j * FF_CHUNK, (j + 1) * FF_CHUNK)
        gate = jnp.dot(hext, wup_ref[:, cs], preferred_element_type=F32)
        up = jnp.dot(hb, wup_ref[:, d_ff + j * FF_CHUNK:d_ff + (j + 1) * FF_CHUNK],
                     preferred_element_type=F32)
        gp = jnp.where(kill_prev, 0.0, pltpu.roll(gate, 1, 0))
        gn = jnp.where(kill_next, 0.0, pltpu.roll(gate, ext - 1, 0))
        conv = cw_ref[0:1, cs] * gp + cw_ref[1:2, cs] * gate + cw_ref[2:3, cs] * gn + cb_ref[:, cs]
        conv = conv[SUBLANES:SUBLANES + tm, :]
        act_sc[:, cs] = (_gelu_tanh(conv) * up).astype(BF16)
    down = jnp.dot(act_sc[...], wdn_ref[...], preferred_element_type=F32)
    o_ref[...] = x + _rms(down, pg_ref[...])


def _ffn(x1, pre_g, wup, conv_w, conv_b, wdn, post_g, seq):
    m, d = x1.shape
    tm = min(FFN_ROW_TILE, seq)
    d_ff = wdn.shape[0]
    nblk8 = m // SUBLANES
    row = lambda i: (i, 0)
    const = lambda i: (0, 0)
    full = lambda a: pl.BlockSpec(a.shape, const, pipeline_mode=pl.Buffered(1))
    return pl.pallas_call(
        functools.partial(_ffn_kernel, tiles_per_seq=seq // tm, d_ff=d_ff),
        grid=(m // tm,),
        in_specs=[
            pl.BlockSpec((tm, d), row),
            pl.BlockSpec((SUBLANES, d), lambda i: (jnp.maximum(i * (tm // SUBLANES) - 1, 0), 0)),
            pl.BlockSpec((SUBLANES, d), lambda i: (jnp.minimum((i + 1) * (tm // SUBLANES), nblk8 - 1), 0)),
            full(pre_g), full(wup), full(conv_w), full(conv_b), full(wdn), full(post_g),
        ],
        out_specs=pl.BlockSpec((tm, d), row),
        out_shape=jax.ShapeDtypeStruct((m, d), F32),
        scratch_shapes=[pltpu.VMEM((tm, d_ff), BF16)],
        compiler_params=_params(("parallel",)),
        name="conv_ffn",
    )(x1, x1, x1, pre_g, wup, conv_w, conv_b, wdn, post_g)


def _rope_tables(seq):
    half = DIFF_QK_DIM // 2
    pos = jnp.arange(seq, dtype=F32)
    inv = ROPE_THETA ** (-jnp.arange(half, dtype=F32) / half)
    ang = pos[:, None] * inv[None, :]
    cos, sin = jnp.cos(ang), jnp.sin(ang)
    reps = LANES // DIFF_QK_DIM
    cos_t = jnp.tile(jnp.concatenate([cos, cos], axis=-1), (1, reps))
    sin_t = jnp.tile(jnp.concatenate([-sin, sin], axis=-1), (1, reps))
    return cos_t, sin_t


def _scan_constants():
    t = jnp.arange(CHUNK)
    tri = jnp.stack([t[:, None] >= t[None, :],
                     t[:, None] <= t[None, :]])
    ch = jnp.arange(LANES)
    bd = (ch[:, None] // RWKV_HEAD_DIM) == (ch[None, :] // RWKV_HEAD_DIM)
    return tri.astype(BF16), bd.astype(BF16)


def _lora_blockdiag(w2):
    z = jnp.zeros_like(w2[0])
    return jnp.concatenate([jnp.concatenate([w2[0], z], axis=1),
                            jnp.concatenate([z, w2[1]], axis=1)], axis=0).astype(BF16)


def kernel(x, pre_mix_norm, post_mix_norm, pre_ffn_norm, post_ffn_norm, w_in, diff_lambda_q1, diff_lambda_k1, diff_lambda_q2, diff_lambda_k2, diff_subln, rwkv_mu, rwkv_w0, rwkv_w2, rwkv_a0, rwkv_a2, rwkv_g2, rwkv_k_k, rwkv_k_a, rwkv_r_k, rwkv_gn_w, rwkv_gn_b, w_out, w_up, ffn_conv_w, ffn_conv_b, w_down):
    batch, seq, d = x.shape
    depth = w_in.shape[0]
    assert seq % ROW_TILE == 0 and seq % ATTN_TK == 0 and ROW_TILE % CHUNK == 0
    m = batch * seq
    nc_total = m // CHUNK
    cos_t, sin_t = _rope_tables(seq)
    tri, bd = _scan_constants()
    x2 = x.reshape(m, d)
    diff_cols = 3 * DIFF_WIDTH
    row2 = lambda a: a.reshape(1, -1)
    for l in range(depth):
        lambda_init = 0.8 - 0.6 * math.exp(-0.3 * l)
        wqk = w_in[l][:, :2 * DIFF_WIDTH].astype(BF16)
        wvt = w_in[l][:, 2 * DIFF_WIDTH:diff_cols].T.astype(BF16)
        wrw = w_in[l][:, diff_cols:].astype(BF16)
        prep_params = (row2(rwkv_mu[l]), rwkv_w0[l], _lora_blockdiag(rwkv_w2[l]), rwkv_a0[l],
                       _lora_blockdiag(rwkv_a2[l]), rwkv_g2[l].astype(BF16), row2(rwkv_k_k[l]),
                       row2(rwkv_k_a[l]), row2(rwkv_r_k[l]), bd, tri)
        q0, q1, k, vt, *prep = _in_proj(x2, row2(pre_mix_norm[l]), wqk, wvt, wrw, cos_t, sin_t, prep_params, seq)
        lam_p = jnp.stack([diff_lambda_q1[l], diff_lambda_k1[l], diff_lambda_q2[l], diff_lambda_k2[l]])
        ya = _attention(lam_p, diff_subln[l].reshape(-1, 1), q0, q1, k, vt, batch, seq, lambda_init)
        vb, g, bonus = prep[0], prep[1], prep[2]
        dir0 = list(prep[3:9]) + [prep[9].reshape(nc_total, 1, RWKV_WIDTH)]
        dir1 = list(prep[10:16]) + [prep[16].reshape(nc_total, 1, RWKV_WIDTH)]
        yf, yb = _wkv(vb, dir0, dir1, batch, seq)
        x2 = _out_proj(x2, ya, yf, yb, bonus, g, row2(rwkv_gn_w[l]), row2(rwkv_gn_b[l]), bd,
                       w_out[l].astype(BF16), row2(post_mix_norm[l]))
        x2 = _ffn(x2, row2(pre_ffn_norm[l]), w_up[l].astype(BF16), ffn_conv_w[l], row2(ffn_conv_b[l]),
                  w_down[l].astype(BF16), row2(post_ffn_norm[l]), seq)
    return x2.reshape(batch, seq, d)
```

```python
import functools
import math

import jax
import jax.numpy as jnp
from jax import lax
from jax.experimental import pallas as pl
from jax.experimental.pallas import tpu as pltpu

F32 = jnp.float32
BF16 = jnp.bfloat16

LANES = 128
SUBLANES = 8
VMEM_LIMIT = 56 * 1024 * 1024

DIFF_QK_DIM = 64
DIFF_V_DIM = 128
DIFF_HEADS = 4
DIFF_WIDTH = DIFF_HEADS * DIFF_V_DIM
RWKV_HEAD_DIM = 64
RWKV_WIDTH = 512
RWKV_PAIRS = RWKV_WIDTH // LANES
DECAY_LORA = 64
ICLR_LORA = 64
GATE_LORA = 128
ROPE_THETA = 10000.0
NORM_EPS = 1e-6
GN_EPS = 64e-5
CHUNK = 64
WKV_SUB = 2

ROW_TILE = 512
FFN_ROW_TILE = 1024
ATTN_TQ = 256
ATTN_TK = 512
ATTN_VROWS = 144
FF_CHUNK = 256


def _rms(x, g):
    return x * lax.rsqrt(jnp.mean(x * x, axis=-1, keepdims=True) + NORM_EPS) * g


def _mm(a, b):
    return jnp.dot(a.astype(BF16), b.astype(BF16), preferred_element_type=F32)


def _mm_nt(a, b):
    return lax.dot_general(a.astype(BF16), b.astype(BF16), (((1,), (1,)), ((), ())),
                           preferred_element_type=F32)


def _mm_tn(a, b):
    return lax.dot_general(a.astype(BF16), b.astype(BF16), (((0,), (0,)), ((), ())),
                           preferred_element_type=F32)


def _split2(x):
    hi = x.astype(BF16)
    lo = (x - hi.astype(F32)).astype(BF16)
    return hi, lo


def _seg_sum(x, bd):
    hi, lo = _split2(x)
    tiles = []
    for s in range(x.shape[1] // LANES):
        sl = slice(s * LANES, (s + 1) * LANES)
        tiles.append(jnp.dot(hi[:, sl], bd, preferred_element_type=F32)
                     + jnp.dot(lo[:, sl], bd, preferred_element_type=F32))
    return jnp.concatenate(tiles, axis=1)


def _chunk_cumsum(x, tri):
    hi, lo = _split2(x)
    chunks = []
    for c in range(x.shape[0] // CHUNK):
        rows = slice(c * CHUNK, (c + 1) * CHUNK)
        chunks.append(jnp.dot(tri, hi[rows], preferred_element_type=F32)
                      + jnp.dot(tri, lo[rows], preferred_element_type=F32))
    return jnp.concatenate(chunks, axis=0)


def _params(semantics):
    return pltpu.CompilerParams(dimension_semantics=semantics, vmem_limit_bytes=VMEM_LIMIT)


def _inproj_kernel(x_ref, xp_ref, xn_ref, g_ref, wqk_ref, wvt_ref, wrw_ref, cos_ref, sin_ref, *rest,
                   tiles_per_seq):
    prep_in, (q0_ref, q1_ref, k_ref, vt_ref), prep_out = rest[:11], rest[11:15], rest[15:]
    tm = x_ref.shape[0]
    hb = _rms(x_ref[...], g_ref[...]).astype(BF16)
    qk = jnp.dot(hb, wqk_ref[...], preferred_element_type=F32)
    cos = cos_ref[...]
    sin = sin_ref[...]
    lane = lax.broadcasted_iota(jnp.int32, (tm, LANES), 1)
    first_half = (lane % DIFF_QK_DIM) < (DIFF_QK_DIM // 2)
    comp0 = lane < DIFF_QK_DIM
    scale = DIFF_QK_DIM ** -0.5
    for j in range(2 * DIFF_HEADS):
        t = qk[:, j * LANES:(j + 1) * LANES]
        partner = jnp.where(first_half, pltpu.roll(t, LANES - DIFF_QK_DIM // 2, 1),
                            pltpu.roll(t, DIFF_QK_DIM // 2, 1))
        o = t * cos + partner * sin
        if j < DIFF_HEADS:
            o = o * scale
            q0_ref[:, j * LANES:(j + 1) * LANES] = jnp.where(comp0, o, 0.0).astype(BF16)
            q1_ref[:, j * LANES:(j + 1) * LANES] = jnp.where(comp0, 0.0, o).astype(BF16)
        else:
            jj = j - DIFF_HEADS
            k_ref[:, jj * LANES:(jj + 1) * LANES] = o.astype(BF16)
    vt = lax.dot_general(wvt_ref[...], hb, (((1,), (1,)), ((), ())), preferred_element_type=F32).astype(BF16)
    pad_rows = ATTN_VROWS - DIFF_V_DIM
    ones_row = lax.broadcasted_iota(jnp.int32, (pad_rows, tm), 0) == 0
    pad = jnp.where(ones_row, 1.0, 0.0).astype(BF16)
    for h in range(DIFF_HEADS):
        vt_ref[h * ATTN_VROWS:h * ATTN_VROWS + DIFF_V_DIM, :] = vt[h * DIFF_V_DIM:(h + 1) * DIFF_V_DIM]
        vt_ref[h * ATTN_VROWS + DIFF_V_DIM:(h + 1) * ATTN_VROWS, :] = pad
    z = jnp.dot(hb, wrw_ref[...], preferred_element_type=F32)
    halo = jnp.concatenate([xp_ref[...], xn_ref[...]], axis=0)
    z_halo = jnp.dot(_rms(halo, g_ref[...]).astype(BF16), wrw_ref[...], preferred_element_type=F32)
    i = pl.program_id(0)
    first = (i % tiles_per_seq) == 0
    last = (i % tiles_per_seq) == tiles_per_seq - 1
    prow = jnp.where(first, 0.0, z_halo[SUBLANES - 1:SUBLANES, :])
    nrow = jnp.where(last, 0.0, z_halo[SUBLANES:SUBLANES + 1, :])
    _rwkv_prep_body(z, prow, nrow, *prep_in, *prep_out)


def _in_proj(x2, g, wqk, wvt, wrw, cos_t, sin_t, prep_params, seq):
    m, d = x2.shape
    tm = ROW_TILE
    tps = seq // tm
    w = RWKV_WIDTH
    nblk8 = m // SUBLANES
    row = lambda i: (i, 0)
    const2 = lambda i: (0, 0)
    const3 = lambda i: (0, 0, 0)
    full = lambda a: pl.BlockSpec(a.shape, const2 if a.ndim == 2 else const3, pipeline_mode=pl.Buffered(1))
    big_bf = jax.ShapeDtypeStruct((m, w), BF16)
    big_f = jax.ShapeDtypeStruct((m, w), F32)
    per_dir_shapes = [big_bf] * 6 + [jax.ShapeDtypeStruct((m // CHUNK, w), F32)]
    per_dir_specs = [pl.BlockSpec((tm, w), row)] * 6 + [pl.BlockSpec((tm // CHUNK, w), row)]
    return pl.pallas_call(
        functools.partial(_inproj_kernel, tiles_per_seq=tps),
        grid=(m // tm,),
        in_specs=[
            pl.BlockSpec((tm, d), row),
            pl.BlockSpec((SUBLANES, d), lambda i: (jnp.maximum(i * (tm // SUBLANES) - 1, 0), 0)),
            pl.BlockSpec((SUBLANES, d), lambda i: (jnp.minimum((i + 1) * (tm // SUBLANES), nblk8 - 1), 0)),
            full(g), full(wqk), full(wvt), full(wrw),
            pl.BlockSpec((tm, LANES), lambda i: (i % tps, 0)),
            pl.BlockSpec((tm, LANES), lambda i: (i % tps, 0)),
        ] + [full(a) for a in prep_params],
        out_specs=[
            pl.BlockSpec((tm, DIFF_WIDTH), row),
            pl.BlockSpec((tm, DIFF_WIDTH), row),
            pl.BlockSpec((tm, DIFF_WIDTH), row),
            pl.BlockSpec((DIFF_HEADS * ATTN_VROWS, tm), lambda i: (i // tps, i % tps)),
        ] + [pl.BlockSpec((tm, w), row)] * 3 + per_dir_specs * 2,
        out_shape=[
            jax.ShapeDtypeStruct((m, DIFF_WIDTH), BF16),
            jax.ShapeDtypeStruct((m, DIFF_WIDTH), BF16),
            jax.ShapeDtypeStruct((m, DIFF_WIDTH), BF16),
            jax.ShapeDtypeStruct((m // seq * DIFF_HEADS * ATTN_VROWS, seq), BF16),
        ] + [big_bf, big_f, big_f] + per_dir_shapes * 2,
        compiler_params=_params(("parallel",)),
        name="in_proj",
    )(x2, x2, x2, g, wqk, wvt, wrw, cos_t, sin_t, *prep_params)


def _attn_kernel(lam_ref, sub_ref, q0_ref, q1_ref, q0n_ref, q1n_ref, k_ref, vt_ref, o_ref,
                 s_sc, mx_sc, m_sc, acc_sc, *, tk, lambda_init):
    seq = k_ref.shape[0]
    nk = seq // tk
    qs = (q0_ref[...], q1_ref[...])

    def scores(qpair, i, slot):
        start = i * tk
        ks = k_ref[pl.ds(start, tk), :]
        maxima = []
        for c in range(2):
            st = lax.dot_general(ks, qpair[c], (((1,), (1,)), ((), ())), preferred_element_type=F32)
            s_sc[slot, c] = st
            maxima.append(jnp.max(st, axis=0, keepdims=True))
        return tuple(maxima)

    @pl.when(pl.program_id(2) == 0)
    def _():
        first = scores(qs, 0, 0)
        mx_sc[0] = first[0]
        mx_sc[1] = first[1]

    m_sc[...] = jnp.full(m_sc.shape, -jnp.inf, F32)
    acc_sc[...] = jnp.zeros(acc_sc.shape, F32)

    def accumulate(i, slot, maxima):
        start = i * tk
        vt = vt_ref[:, pl.ds(start, tk)]
        for c in range(2):
            m_old = m_sc[c]
            m_new = jnp.maximum(m_old, maxima[c])
            alpha = jnp.exp(m_old - m_new)
            p = jnp.exp(s_sc[slot, c] - m_new).astype(BF16)
            acc_sc[c] = alpha * acc_sc[c] + jnp.dot(vt, p, preferred_element_type=F32)
            m_sc[c] = m_new

    maxima = (mx_sc[0], mx_sc[1])
    for blk in range(nk):
        if blk + 1 < nk:
            ahead = scores(qs, blk + 1, (blk + 1) % 2)
        else:
            ahead = scores((q0n_ref[...], q1n_ref[...]), 0, 0)
        accumulate(blk, blk % 2, maxima)
        maxima = ahead
    mx_sc[0] = maxima[0]
    mx_sc[1] = maxima[1]

    lp = lam_ref[...]
    lam = (jnp.exp(jnp.sum(lp[0:1] * lp[1:2], axis=-1, keepdims=True))
           - jnp.exp(jnp.sum(lp[2:3] * lp[3:4], axis=-1, keepdims=True)) + lambda_init)
    a0 = acc_sc[0]
    a1 = acc_sc[1]
    dv = DIFF_V_DIM
    o = a0[:dv] / a0[dv:dv + 1] - lam * (a1[:dv] / a1[dv:dv + 1])
    o = o * lax.rsqrt(jnp.mean(o * o, axis=0, keepdims=True) + NORM_EPS) * sub_ref[...] * (1.0 - lambda_init)
    o_ref[...] = o.T.astype(BF16)


def _attention(lam_p, subln_col, q0, q1, k, vt, batch, seq, lambda_init):
    tq, tk = ATTN_TQ, min(ATTN_TK, seq // 2)
    assert (seq // tk) % 2 == 0
    nq = seq // tq
    qmap = lambda b, h, i: (b * nq + i, h)
    qnext = lambda b, h, i: (b * nq + jnp.minimum(i + 1, nq - 1), h)
    const = lambda b, h, i: (0, 0)
    return pl.pallas_call(
        functools.partial(_attn_kernel, tk=tk, lambda_init=lambda_init),
        grid=(batch, DIFF_HEADS, nq),
        in_specs=[
            pl.BlockSpec(lam_p.shape, const),
            pl.BlockSpec(subln_col.shape, const),
            pl.BlockSpec((tq, LANES), qmap),
            pl.BlockSpec((tq, LANES), qmap),
            pl.BlockSpec((tq, LANES), qnext),
            pl.BlockSpec((tq, LANES), qnext),
            pl.BlockSpec((seq, LANES), lambda b, h, i: (b, h)),
            pl.BlockSpec((ATTN_VROWS, seq), lambda b, h, i: (b * DIFF_HEADS + h, 0)),
        ],
        out_specs=pl.BlockSpec((tq, LANES), qmap),
        out_shape=jax.ShapeDtypeStruct((batch * seq, DIFF_WIDTH), BF16),
        scratch_shapes=[
            pltpu.VMEM((2, 2, tk, tq), F32),
            pltpu.VMEM((2, 1, tq), F32),
            pltpu.VMEM((2, 1, tq), F32),
            pltpu.VMEM((2, ATTN_VROWS, tq), F32),
        ],
        compiler_params=_params(("parallel", "parallel", "arbitrary")),
        name="diff_attn",
    )(lam_p, subln_col, q0, q1, q0, q1, k, vt)


def _sigmoid(x):
    return 0.5 * jnp.tanh(0.5 * x) + 0.5


def _rwkv_prep_body(z, prow, nrow, mu_ref, w0_ref, w2_ref, a0_ref, a2_ref, g2_ref,
                    kk_ref, ka_ref, rk_ref, bd_ref, tri_ref,
                    v_o, g_o, bonus_o,
                    at0_o, rt0_o, bb0_o, kb0_o, bh0_o, kh0_o, pl0_o,
                    at1_o, rt1_o, bb1_o, kb1_o, bh1_o, kh1_o, pl1_o):
    tm = z.shape[0]
    w = RWKV_WIDTH
    row = lax.broadcasted_iota(jnp.int32, (tm, 1), 0)
    zp = jnp.where(row == 0, prow, pltpu.roll(z, 1, 0))
    zn = jnp.where(row == tm - 1, nrow, pltpu.roll(z, tm - 1, 0))
    zs = z + (0.5 * (zp + zn) - z) * mu_ref[...]

    r = zs[:, 0:w]
    k = zs[:, w:2 * w]
    v = zs[:, 2 * w:3 * w]
    o3 = 3 * w
    wd = zs[:, o3:o3 + 2 * DECAY_LORA]
    ad = zs[:, o3 + 2 * DECAY_LORA:o3 + 2 * DECAY_LORA + 2 * ICLR_LORA]
    gd = zs[:, o3 + 2 * DECAY_LORA + 2 * ICLR_LORA:]
    bd = bd_ref[...]

    g_o[...] = _mm(_sigmoid(gd), g2_ref[...])
    v_o[...] = v.astype(BF16)
    kkf = k * kk_ref[...]
    kk = kkf * lax.rsqrt(jnp.maximum(_seg_sum(kkf * kkf, bd), 1e-12))
    wl_all = _mm(jnp.tanh(wd), w2_ref[...])
    al_all = _mm(ad, a2_ref[...])
    outs = ((at0_o, rt0_o, bb0_o, kb0_o, bh0_o, kh0_o, pl0_o),
            (at1_o, rt1_o, bb1_o, kb1_o, bh1_o, kh1_o, pl1_o))
    bonus = jnp.zeros((tm, w), F32)
    for d in range(2):
        at_o, rt_o, bb_o, kb_o, bh_o, kh_o, pl_o = outs[d]
        lw = -math.exp(-0.5) * _sigmoid(w0_ref[d:d + 1, :] + wl_all[:, d * w:(d + 1) * w])
        a = _sigmoid(a0_ref[d:d + 1, :] + al_all[:, d * w:(d + 1) * w])
        kd = k * (1.0 + (a - 1.0) * ka_ref[...])
        bonus = bonus + _seg_sum(r * kd * rk_ref[...], bd) * v
        cin = _chunk_cumsum(lw, tri_ref[d])
        cin3 = cin.reshape(tm // CHUNK, CHUNK, w)
        tot3 = cin3[:, CHUNK - 1:CHUNK, :] if d == 0 else cin3[:, 0:1, :]
        rem = (tot3 - cin3).reshape(tm, w)
        tot = tot3.reshape(tm // CHUNK, w)
        e_in = jnp.exp(cin)
        e_neg = jnp.exp(-cin)
        e_rem = jnp.exp(rem)
        beta = kk * a
        at_o[...] = (-kk * jnp.exp(cin - lw)).astype(BF16)
        rt_o[...] = (r * e_in).astype(BF16)
        bb_o[...] = (beta * e_neg).astype(BF16)
        kb_o[...] = (kd * e_neg).astype(BF16)
        bh_o[...] = (beta * e_rem).astype(BF16)
        kh_o[...] = (kd * e_rem).astype(BF16)
        pl_o[...] = jnp.exp(tot)
    bonus_o[...] = bonus


def _wkv_kernel(v0_ref, at0, rt0, bb0, kb0, bh0, kh0, pl0,
                v1_ref, at1, rt1, bb1, kb1, bh1, kh1, pl1,
                y0_ref, y1_ref, s_sc):
    c = pl.program_id(1)

    @pl.when(c == 0)
    def _():
        s_sc[...] = jnp.zeros(s_sc.shape, F32)

    n = 2 * CHUNK
    ri = lax.broadcasted_iota(jnp.int32, (n, n), 0)
    ci = lax.broadcasted_iota(jnp.int32, (n, n), 1)
    ti, tj = ri % CHUNK, ci % CHUNK
    same = (ri // CHUNK) == (ci // CHUNK)
    eye = ri == ci
    head0 = lax.broadcasted_iota(jnp.int32, (CHUNK, LANES), 1) < RWKV_HEAD_DIM

    def stack(x):
        zero = jnp.zeros_like(x)
        return jnp.concatenate([jnp.where(head0, x, zero), jnp.where(head0, zero, x)], axis=0)

    dirs = ((v0_ref, at0, rt0, bb0, kb0, bh0, kh0, pl0, y0_ref),
            (v1_ref, at1, rt1, bb1, kb1, bh1, kh1, pl1, y1_ref))
    units = []
    for d, (v_ref, at_r, rt_r, bb_r, kb_r, bh_r, kh_r, pl_r, y_ref) in enumerate(dirs):
        before = (ti > tj) if d == 0 else (ti < tj)
        strict = same & before
        incl = strict | eye
        levels = []
        b = 1
        while b < CHUNK:
            blk = same & ((ti // (2 * b)) == (tj // (2 * b)))
            hi, hj = (ti // b) % 2, (tj // b) % 2
            levels.append(blk & ((hi == 1) & (hj == 0) if d == 0 else (hi == 0) & (hj == 1)))
            b *= 2
        for p in range(RWKV_PAIRS):
            for j in (range(WKV_SUB) if d == 0 else range(WKV_SUB - 1, -1, -1)):
                units.append((d, p, slice(p * LANES, (p + 1) * LANES), strict, incl, levels,
                              (v_ref, at_r, rt_r, bb_r, kb_r, bh_r, kh_r, pl_r, y_ref),
                              j, slice(j * CHUNK, (j + 1) * CHUNK)))
    nu = range(len(units))

    def stacked(k):
        return [stack(u[6][k][u[8], u[2]]) for u in units]

    v_s, at_s, rt_s, bb_s, kb_s, bh_s, kh_s = (stacked(k) for k in range(7))
    gram = [_mm_nt(jnp.concatenate([at_s[i], rt_s[i]], axis=0), jnp.concatenate([bb_s[i], kb_s[i]], axis=0))
            for i in nu]
    a_ab = [jnp.where(units[i][3], gram[i][:n, :n], 0.0) for i in nu]
    a_ak = [jnp.where(units[i][3], gram[i][:n, n:], 0.0).astype(BF16) for i in nu]
    a_rb = [jnp.where(units[i][4], gram[i][n:, :n], 0.0).astype(BF16) for i in nu]
    a_rk = [jnp.where(units[i][4], gram[i][n:, n:], 0.0).astype(BF16) for i in nu]
    t_inv = [jnp.where(eye, 1.0, jnp.where(units[i][5][0], a_ab[i], 0.0)) for i in nu]
    for lv in range(1, len(units[0][5])):
        t_b = [t.astype(BF16) for t in t_inv]
        half = [_mm(t_b[i], jnp.where(units[i][5][lv], a_ab[i], 0.0)) for i in nu]
        t_inv = [t_inv[i] + _mm(half[i], t_b[i]) for i in nu]
    t_b = [t.astype(BF16) for t in t_inv]
    av = [_mm(jnp.concatenate([a_ak[i], a_rk[i]], axis=0), v_s[i]) for i in nu]
    akv = [av[i][:n].astype(BF16) for i in nu]
    wu = [_mm(t_b[i], jnp.concatenate([at_s[i], akv[i]], axis=1)).astype(BF16) for i in nu]
    qy = [_mm(a_rb[i], wu[i]) for i in nu]
    y0 = [qy[i][:, LANES:] + av[i][n:] for i in nu]
    qt = [rt_s[i].astype(F32) + qy[i][:, :LANES] for i in nu]
    mn = [_mm_tn(wu[i], bh_s[i]) for i in nu]
    n0t = [mn[i][LANES:, :] + _mm_tn(v_s[i], kh_s[i]) for i in nu]
    state = {(d, p): s_sc[d, p] for d in range(2) for p in range(RWKV_PAIRS)}
    for k in range(WKV_SUB):
        for i in nu:
            if i % WKV_SUB != k:
                continue
            d, p, sl, rows = units[i][0], units[i][1], units[i][2], units[i][8]
            pl_r, y_ref = units[i][6][7], units[i][6][8]
            st = state[d, p]
            ys = _mm_nt(qt[i], st) + y0[i]
            y_ref[rows, sl] = ys[:CHUNK] + ys[CHUNK:]
            state[d, p] = st * pl_r[units[i][7], :, sl] + _mm(st, mn[i][:LANES, :]) + n0t[i]
    for (d, p), st in state.items():
        s_sc[d, p] = st


def _wkv(v, dir0, dir1, batch, seq):
    m, w = v.shape
    nc = seq // (CHUNK * WKV_SUB)
    fmap = lambda b, c: (b * nc + c, 0)
    bmap = lambda b, c: (b * nc + nc - 1 - c, 0)
    fmap3 = lambda b, c: (b * nc + c, 0, 0)
    bmap3 = lambda b, c: (b * nc + nc - 1 - c, 0, 0)
    blk = lambda im: pl.BlockSpec((CHUNK * WKV_SUB, w), im)
    in_specs = ([blk(fmap)] * 7 + [pl.BlockSpec((WKV_SUB, 1, w), fmap3)]
                + [blk(bmap)] * 7 + [pl.BlockSpec((WKV_SUB, 1, w), bmap3)])
    return pl.pallas_call(
        _wkv_kernel,
        grid=(batch, nc),
        in_specs=in_specs,
        out_specs=[blk(fmap), blk(bmap)],
        out_shape=[jax.ShapeDtypeStruct((m, w), F32)] * 2,
        scratch_shapes=[pltpu.VMEM((2, RWKV_PAIRS, LANES, LANES), F32)],
        compiler_params=_params(("parallel", "arbitrary")),
        name="wkv_scan",
    )(v, *dir0, v, *dir1)


def _outproj_kernel(x_ref, ya_ref, yf_ref, yb_ref, bonus_ref, g_ref, gnw_ref, gnb_ref, bd_ref,
                    wout_ref, pg_ref, o_ref):
    bd = bd_ref[...]
    inv_n = 1.0 / RWKV_HEAD_DIM
    y = yf_ref[...] + yb_ref[...]
    mean = _seg_sum(y, bd) * inv_n
    yc = y - mean
    var = _seg_sum(yc * yc, bd) * inv_n
    yn = yc * lax.rsqrt(var + GN_EPS) * gnw_ref[...] + gnb_ref[...]
    yb = ((yn + bonus_ref[...]) * g_ref[...]).astype(BF16)
    mix = jnp.dot(jnp.concatenate([ya_ref[...], yb], axis=1), wout_ref[...], preferred_element_type=F32)
    o_ref[...] = x_ref[...] + _rms(mix, pg_ref[...])


def _out_proj(x2, ya, yf, yb, bonus, g, gn_w, gn_b, bd, wout, post_g):
    m, d = x2.shape
    tm = ROW_TILE
    w = RWKV_WIDTH
    row = lambda i: (i, 0)
    const = lambda i: (0, 0)
    full = lambda a: pl.BlockSpec(a.shape, const, pipeline_mode=pl.Buffered(1))
    return pl.pallas_call(
        _outproj_kernel,
        grid=(m // tm,),
        in_specs=[pl.BlockSpec((tm, d), row), pl.BlockSpec((tm, DIFF_WIDTH), row)]
                 + [pl.BlockSpec((tm, w), row)] * 4
                 + [full(gn_w), full(gn_b), full(bd), full(wout), full(post_g)],
        out_specs=pl.BlockSpec((tm, d), row),
        out_shape=jax.ShapeDtypeStruct((m, d), F32),
        compiler_params=_params(("parallel",)),
        name="out_proj",
    )(x2, ya, yf, yb, bonus, g, gn_w, gn_b, bd, wout, post_g)


def _gelu_tanh(x):
    c = math.sqrt(2.0 / math.pi)
    return 0.5 * x * (1.0 + jnp.tanh(c * (x + 0.044715 * (x * x * x))))


def _ffn_kernel(x_ref, xp_ref, xn_ref, g_ref, wup_ref, cw_ref, cb_ref, wdn_ref, pg_ref, o_ref, act_sc,
                *, tiles_per_seq, d_ff):
    tm = x_ref.shape[0]
    i = pl.program_id(0)
    first = (i % tiles_per_seq) == 0
    last = (i % tiles_per_seq) == tiles_per_seq - 1
    g = g_ref[...]
    x = x_ref[...]
    hb = _rms(x, g).astype(BF16)
    hp = _rms(xp_ref[...], g).astype(BF16)
    hn = _rms(xn_ref[...], g).astype(BF16)
    hext = jnp.concatenate([hp, hb, hn], axis=0)
    ext = tm + 2 * SUBLANES
    row = lax.broadcasted_iota(jnp.int32, (ext, 1), 0)
    kill_prev = jnp.logical_and(first, row == SUBLANES)
    kill_next = jnp.logical_and(last, row == SUBLANES + tm - 1)
    for j in range(d_ff // FF_CHUNK):
        cs = slice(j * FF_CHUNK, (j + 1) * FF_CHUNK)
        gate = jnp.dot(hext, wup_ref[:, cs], preferred_element_type=F32)
        up = jnp.dot(hb, wup_ref[:, d_ff + j * FF_CHUNK:d_ff + (j + 1) * FF_CHUNK],
                     preferred_element_type=F32)
        gp = jnp.where(kill_prev, 0.0, pltpu.roll(gate, 1, 0))
        gn = jnp.where(kill_next, 0.0, pltpu.roll(gate, ext - 1, 0))
        conv = cw_ref[0:1, cs] * gp + cw_ref[1:2, cs] * gate + cw_ref[2:3, cs] * gn + cb_ref[:, cs]
        conv = conv[SUBLANES:SUBLANES + tm, :]
        act_sc[:, cs] = (_gelu_tanh(conv) * up).astype(BF16)
    down = jnp.dot(act_sc[...], wdn_ref[...], preferred_element_type=F32)
    o_ref[...] = x + _rms(down, pg_ref[...])


def _ffn(x1, pre_g, wup, conv_w, conv_b, wdn, post_g, seq):
    m, d = x1.shape
    tm = min(FFN_ROW_TILE, seq)
    d_ff = wdn.shape[0]
    nblk8 = m // SUBLANES
    row = lambda i: (i, 0)
    const = lambda i: (0, 0)
    full = lambda a: pl.BlockSpec(a.shape, const, pipeline_mode=pl.Buffered(1))
    return pl.pallas_call(
        functools.partial(_ffn_kernel, tiles_per_seq=seq // tm, d_ff=d_ff),
        grid=(m // tm,),
        in_specs=[
            pl.BlockSpec((tm, d), row),
            pl.BlockSpec((SUBLANES, d), lambda i: (jnp.maximum(i * (tm // SUBLANES) - 1, 0), 0)),
            pl.BlockSpec((SUBLANES, d), lambda i: (jnp.minimum((i + 1) * (tm // SUBLANES), nblk8 - 1), 0)),
            full(pre_g), full(wup), full(conv_w), full(conv_b), full(wdn), full(post_g),
        ],
        out_specs=pl.BlockSpec((tm, d), row),
        out_shape=jax.ShapeDtypeStruct((m, d), F32),
        scratch_shapes=[pltpu.VMEM((tm, d_ff), BF16)],
        compiler_params=_params(("parallel",)),
        name="conv_ffn",
    )(x1, x1, x1, pre_g, wup, conv_w, conv_b, wdn, post_g)


def _rope_tables(seq):
    half = DIFF_QK_DIM // 2
    pos = jnp.arange(seq, dtype=F32)
    inv = ROPE_THETA ** (-jnp.arange(half, dtype=F32) / half)
    ang = pos[:, None] * inv[None, :]
    cos, sin = jnp.cos(ang), jnp.sin(ang)
    reps = LANES // DIFF_QK_DIM
    cos_t = jnp.tile(jnp.concatenate([cos, cos], axis=-1), (1, reps))
    sin_t = jnp.tile(jnp.concatenate([-sin, sin], axis=-1), (1, reps))
    return cos_t, sin_t


def _scan_constants():
    t = jnp.arange(CHUNK)
    tri = jnp.stack([t[:, None] >= t[None, :],
                     t[:, None] <= t[None, :]])
    ch = jnp.arange(LANES)
    bd = (ch[:, None] // RWKV_HEAD_DIM) == (ch[None, :] // RWKV_HEAD_DIM)
    return tri.astype(BF16), bd.astype(BF16)


def _lora_blockdiag(w2):
    z = jnp.zeros_like(w2[0])
    return jnp.concatenate([jnp.concatenate([w2[0], z], axis=1),
                            jnp.concatenate([z, w2[1]], axis=1)], axis=0).astype(BF16)


def kernel(x, pre_mix_norm, post_mix_norm, pre_ffn_norm, post_ffn_norm, w_in, diff_lambda_q1, diff_lambda_k1, diff_lambda_q2, diff_lambda_k2, diff_subln, rwkv_mu, rwkv_w0, rwkv_w2, rwkv_a0, rwkv_a2, rwkv_g2, rwkv_k_k, rwkv_k_a, rwkv_r_k, rwkv_gn_w, rwkv_gn_b, w_out, w_up, ffn_conv_w, ffn_conv_b, w_down):
    batch, seq, d = x.shape
    depth = w_in.shape[0]
    assert seq % ROW_TILE == 0 and seq % ATTN_TK == 0 and ROW_TILE % CHUNK == 0
    m = batch * seq
    nc_total = m // CHUNK
    cos_t, sin_t = _rope_tables(seq)
    tri, bd = _scan_constants()
    x2 = x.reshape(m, d)
    diff_cols = 3 * DIFF_WIDTH
    row2 = lambda a: a.reshape(1, -1)
    for l in range(depth):
        lambda_init = 0.8 - 0.6 * math.exp(-0.3 * l)
        wqk = w_in[l][:, :2 * DIFF_WIDTH].astype(BF16)
        wvt = w_in[l][:, 2 * DIFF_WIDTH:diff_cols].T.astype(BF16)
        wrw = w_in[l][:, diff_cols:].astype(BF16)
        prep_params = (row2(rwkv_mu[l]), rwkv_w0[l], _lora_blockdiag(rwkv_w2[l]), rwkv_a0[l],
                       _lora_blockdiag(rwkv_a2[l]), rwkv_g2[l].astype(BF16), row2(rwkv_k_k[l]),
                       row2(rwkv_k_a[l]), row2(rwkv_r_k[l]), bd, tri)
        q0, q1, k, vt, *prep = _in_proj(x2, row2(pre_mix_norm[l]), wqk, wvt, wrw, cos_t, sin_t, prep_params, seq)
        lam_p = jnp.stack([diff_lambda_q1[l], diff_lambda_k1[l], diff_lambda_q2[l], diff_lambda_k2[l]])
        ya = _attention(lam_p, diff_subln[l].reshape(-1, 1), q0, q1, k, vt, batch, seq, lambda_init)
        vb, g, bonus = prep[0], prep[1], prep[2]
        dir0 = list(prep[3:9]) + [prep[9].reshape(nc_total, 1, RWKV_WIDTH)]
        dir1 = list(prep[10:16]) + [prep[16].reshape(nc_total, 1, RWKV_WIDTH)]
        yf, yb = _wkv(vb, dir0, dir1, batch, seq)
        x2 = _out_proj(x2, ya, yf, yb, bonus, g, row2(rwkv_gn_w[l]), row2(rwkv_gn_b[l]), bd,
                       w_out[l].astype(BF16), row2(post_mix_norm[l]))
        x2 = _ffn(x2, row2(pre_ffn_norm[l]), w_up[l].astype(BF16), ffn_conv_w[l], row2(ffn_conv_b[l]),
                  w_down[l].astype(BF16), row2(post_ffn_norm[l]), seq)
    return x2.reshape(batch, seq, d)
```

```python
import functools
import math

import jax
import jax.numpy as jnp
from jax import lax
from jax.experimental import pallas as pl
from jax.experimental.pallas import tpu as pltpu

F32 = jnp.float32
BF16 = jnp.bfloat16

LANES = 128
SUBLANES = 8
VMEM_LIMIT = 56 * 1024 * 1024

DIFF_QK_DIM = 64
DIFF_V_DIM = 128
DIFF_HEADS = 4
DIFF_WIDTH = DIFF_HEADS * DIFF_V_DIM
RWKV_HEAD_DIM = 64
RWKV_WIDTH = 512
RWKV_PAIRS = RWKV_WIDTH // LANES
DECAY_LORA = 64
ICLR_LORA = 64
GATE_LORA = 128
ROPE_THETA = 10000.0
NORM_EPS = 1e-6
GN_EPS = 64e-5
CHUNK = 64
WKV_SUB = 4

ROW_TILE = 512
FFN_ROW_TILE = 1024
ATTN_TQ = 256
ATTN_TK = 512
ATTN_VROWS = 144
FF_CHUNK = 256


def _rms(x, g):
    return x * lax.rsqrt(jnp.mean(x * x, axis=-1, keepdims=True) + NORM_EPS) * g


def _mm(a, b):
    return jnp.dot(a.astype(BF16), b.astype(BF16), preferred_element_type=F32)


def _mm_nt(a, b):
    return lax.dot_general(a.astype(BF16), b.astype(BF16), (((1,), (1,)), ((), ())),
                           preferred_element_type=F32)


def _mm_tn(a, b):
    return lax.dot_general(a.astype(BF16), b.astype(BF16), (((0,), (0,)), ((), ())),
                           preferred_element_type=F32)


def _split2(x):
    hi = x.astype(BF16)
    lo = (x - hi.astype(F32)).astype(BF16)
    return hi, lo


def _seg_sum(x, bd):
    hi, lo = _split2(x)
    tiles = []
    for s in range(x.shape[1] // LANES):
        sl = slice(s * LANES, (s + 1) * LANES)
        tiles.append(jnp.dot(hi[:, sl], bd, preferred_element_type=F32)
                     + jnp.dot(lo[:, sl], bd, preferred_element_type=F32))
    return jnp.concatenate(tiles, axis=1)


def _chunk_cumsum(x, tri):
    hi, lo = _split2(x)
    chunks = []
    for c in range(x.shape[0] // CHUNK):
        rows = slice(c * CHUNK, (c + 1) * CHUNK)
        chunks.append(jnp.dot(tri, hi[rows], preferred_element_type=F32)
                      + jnp.dot(tri, lo[rows], preferred_element_type=F32))
    return jnp.concatenate(chunks, axis=0)


def _params(semantics):
    return pltpu.CompilerParams(dimension_semantics=semantics, vmem_limit_bytes=VMEM_LIMIT)


def _inproj_kernel(x_ref, xp_ref, xn_ref, g_ref, wqk_ref, wvt_ref, wrw_ref, cos_ref, sin_ref, *rest,
                   tiles_per_seq):
    prep_in, (q0_ref, q1_ref, k_ref, vt_ref), prep_out = rest[:11], rest[11:15], rest[15:]
    tm = x_ref.shape[0]
    hb = _rms(x_ref[...], g_ref[...]).astype(BF16)
    qk = jnp.dot(hb, wqk_ref[...], preferred_element_type=F32)
    cos = cos_ref[...]
    sin = sin_ref[...]
    lane = lax.broadcasted_iota(jnp.int32, (tm, LANES), 1)
    first_half = (lane % DIFF_QK_DIM) < (DIFF_QK_DIM // 2)
    comp0 = lane < DIFF_QK_DIM
    scale = DIFF_QK_DIM ** -0.5
    for j in range(2 * DIFF_HEADS):
        t = qk[:, j * LANES:(j + 1) * LANES]
        partner = jnp.where(first_half, pltpu.roll(t, LANES - DIFF_QK_DIM // 2, 1),
                            pltpu.roll(t, DIFF_QK_DIM // 2, 1))
        o = t * cos + partner * sin
        if j < DIFF_HEADS:
            o = o * scale
            q0_ref[:, j * LANES:(j + 1) * LANES] = jnp.where(comp0, o, 0.0).astype(BF16)
            q1_ref[:, j * LANES:(j + 1) * LANES] = jnp.where(comp0, 0.0, o).astype(BF16)
        else:
            jj = j - DIFF_HEADS
            k_ref[:, jj * LANES:(jj + 1) * LANES] = o.astype(BF16)
    vt = lax.dot_general(wvt_ref[...], hb, (((1,), (1,)), ((), ())), preferred_element_type=F32).astype(BF16)
    pad_rows = ATTN_VROWS - DIFF_V_DIM
    ones_row = lax.broadcasted_iota(jnp.int32, (pad_rows, tm), 0) == 0
    pad = jnp.where(ones_row, 1.0, 0.0).astype(BF16)
    for h in range(DIFF_HEADS):
        vt_ref[h * ATTN_VROWS:h * ATTN_VROWS + DIFF_V_DIM, :] = vt[h * DIFF_V_DIM:(h + 1) * DIFF_V_DIM]
        vt_ref[h * ATTN_VROWS + DIFF_V_DIM:(h + 1) * ATTN_VROWS, :] = pad
    z = jnp.dot(hb, wrw_ref[...], preferred_element_type=F32)
    halo = jnp.concatenate([xp_ref[...], xn_ref[...]], axis=0)
    z_halo = jnp.dot(_rms(halo, g_ref[...]).astype(BF16), wrw_ref[...], preferred_element_type=F32)
    i = pl.program_id(0)
    first = (i % tiles_per_seq) == 0
    last = (i % tiles_per_seq) == tiles_per_seq - 1
    prow = jnp.where(first, 0.0, z_halo[SUBLANES - 1:SUBLANES, :])
    nrow = jnp.where(last, 0.0, z_halo[SUBLANES:SUBLANES + 1, :])
    _rwkv_prep_body(z, prow, nrow, *prep_in, *prep_out)


def _in_proj(x2, g, wqk, wvt, wrw, cos_t, sin_t, prep_params, seq):
    m, d = x2.shape
    tm = ROW_TILE
    tps = seq // tm
    w = RWKV_WIDTH
    nblk8 = m // SUBLANES
    row = lambda i: (i, 0)
    const2 = lambda i: (0, 0)
    const3 = lambda i: (0, 0, 0)
    full = lambda a: pl.BlockSpec(a.shape, const2 if a.ndim == 2 else const3, pipeline_mode=pl.Buffered(1))
    big_bf = jax.ShapeDtypeStruct((m, w), BF16)
    big_f = jax.ShapeDtypeStruct((m, w), F32)
    per_dir_shapes = [big_bf] * 6 + [jax.ShapeDtypeStruct((m // CHUNK, w), F32)]
    per_dir_specs = [pl.BlockSpec((tm, w), row)] * 6 + [pl.BlockSpec((tm // CHUNK, w), row)]
    return pl.pallas_call(
        functools.partial(_inproj_kernel, tiles_per_seq=tps),
        grid=(m // tm,),
        in_specs=[
            pl.BlockSpec((tm, d), row),
            pl.BlockSpec((SUBLANES, d), lambda i: (jnp.maximum(i * (tm // SUBLANES) - 1, 0), 0)),
            pl.BlockSpec((SUBLANES, d), lambda i: (jnp.minimum((i + 1) * (tm // SUBLANES), nblk8 - 1), 0)),
            full(g), full(wqk), full(wvt), full(wrw),
            pl.BlockSpec((tm, LANES), lambda i: (i % tps, 0)),
            pl.BlockSpec((tm, LANES), lambda i: (i % tps, 0)),
        ] + [full(a) for a in prep_params],
        out_specs=[
            pl.BlockSpec((tm, DIFF_WIDTH), row),
            pl.BlockSpec((tm, DIFF_WIDTH), row),
            pl.BlockSpec((tm, DIFF_WIDTH), row),
            pl.BlockSpec((DIFF_HEADS * ATTN_VROWS, tm), lambda i: (i // tps, i % tps)),
        ] + [pl.BlockSpec((tm, w), row)] * 3 + per_dir_specs * 2,
        out_shape=[
            jax.ShapeDtypeStruct((m, DIFF_WIDTH), BF16),
            jax.ShapeDtypeStruct((m, DIFF_WIDTH), BF16),
            jax.ShapeDtypeStruct((m, DIFF_WIDTH), BF16),
            jax.ShapeDtypeStruct((m // seq * DIFF_HEADS * ATTN_VROWS, seq), BF16),
        ] + [big_bf, big_f, big_f] + per_dir_shapes * 2,
        compiler_params=_params(("parallel",)),
        name="in_proj",
    )(x2, x2, x2, g, wqk, wvt, wrw, cos_t, sin_t, *prep_params)


def _attn_kernel(lam_ref, sub_ref, q0_ref, q1_ref, q0n_ref, q1n_ref, k_ref, vt_ref, o_ref,
                 s_sc, mx_sc, m_sc, acc_sc, *, tk, lambda_init):
    seq = k_ref.shape[0]
    nk = seq // tk
    qs = (q0_ref[...], q1_ref[...])

    def scores(qpair, i, slot):
        start = i * tk
        ks = k_ref[pl.ds(start, tk), :]
        maxima = []
        for c in range(2):
            st = lax.dot_general(ks, qpair[c], (((1,), (1,)), ((), ())), preferred_element_type=F32)
            s_sc[slot, c] = st
            maxima.append(jnp.max(st, axis=0, keepdims=True))
        return tuple(maxima)

    @pl.when(pl.program_id(2) == 0)
    def _():
        first = scores(qs, 0, 0)
        mx_sc[0] = first[0]
        mx_sc[1] = first[1]

    m_sc[...] = jnp.full(m_sc.shape, -jnp.inf, F32)
    acc_sc[...] = jnp.zeros(acc_sc.shape, F32)

    def accumulate(i, slot, maxima):
        start = i * tk
        vt = vt_ref[:, pl.ds(start, tk)]
        for c in range(2):
            m_old = m_sc[c]
            m_new = jnp.maximum(m_old, maxima[c])
            alpha = jnp.exp(m_old - m_new)
            p = jnp.exp(s_sc[slot, c] - m_new).astype(BF16)
            acc_sc[c] = alpha * acc_sc[c] + jnp.dot(vt, p, preferred_element_type=F32)
            m_sc[c] = m_new

    maxima = (mx_sc[0], mx_sc[1])
    for blk in range(nk):
        if blk + 1 < nk:
            ahead = scores(qs, blk + 1, (blk + 1) % 2)
        else:
            ahead = scores((q0n_ref[...], q1n_ref[...]), 0, 0)
        accumulate(blk, blk % 2, maxima)
        maxima = ahead
    mx_sc[0] = maxima[0]
    mx_sc[1] = maxima[1]

    lp = lam_ref[...]
    lam = (jnp.exp(jnp.sum(lp[0:1] * lp[1:2], axis=-1, keepdims=True))
           - jnp.exp(jnp.sum(lp[2:3] * lp[3:4], axis=-1, keepdims=True)) + lambda_init)
    a0 = acc_sc[0]
    a1 = acc_sc[1]
    dv = DIFF_V_DIM
    o = a0[:dv] / a0[dv:dv + 1] - lam * (a1[:dv] / a1[dv:dv + 1])
    o = o * lax.rsqrt(jnp.mean(o * o, axis=0, keepdims=True) + NORM_EPS) * sub_ref[...] * (1.0 - lambda_init)
    o_ref[...] = o.T.astype(BF16)


def _attention(lam_p, subln_col, q0, q1, k, vt, batch, seq, lambda_init):
    tq, tk = ATTN_TQ, min(ATTN_TK, seq // 2)
    assert (seq // tk) % 2 == 0
    nq = seq // tq
    qmap = lambda b, h, i: (b * nq + i, h)
    qnext = lambda b, h, i: (b * nq + jnp.minimum(i + 1, nq - 1), h)
    const = lambda b, h, i: (0, 0)
    return pl.pallas_call(
        functools.partial(_attn_kernel, tk=tk, lambda_init=lambda_init),
        grid=(batch, DIFF_HEADS, nq),
        in_specs=[
            pl.BlockSpec(lam_p.shape, const),
            pl.BlockSpec(subln_col.shape, const),
            pl.BlockSpec((tq, LANES), qmap),
            pl.BlockSpec((tq, LANES), qmap),
            pl.BlockSpec((tq, LANES), qnext),
            pl.BlockSpec((tq, LANES), qnext),
            pl.BlockSpec((seq, LANES), lambda b, h, i: (b, h)),
            pl.BlockSpec((ATTN_VROWS, seq), lambda b, h, i: (b * DIFF_HEADS + h, 0)),
        ],
        out_specs=pl.BlockSpec((tq, LANES), qmap),
        out_shape=jax.ShapeDtypeStruct((batch * seq, DIFF_WIDTH), BF16),
        scratch_shapes=[
            pltpu.VMEM((2, 2, tk, tq), F32),
            pltpu.VMEM((2, 1, tq), F32),
            pltpu.VMEM((2, 1, tq), F32),
            pltpu.VMEM((2, ATTN_VROWS, tq), F32),
        ],
        compiler_params=_params(("parallel", "parallel", "arbitrary")),
        name="diff_attn",
    )(lam_p, subln_col, q0, q1, q0, q1, k, vt)


def _sigmoid(x):
    return 0.5 * jnp.tanh(0.5 * x) + 0.5


def _rwkv_prep_body(z, prow, nrow, mu_ref, w0_ref, w2_ref, a0_ref, a2_ref, g2_ref,
                    kk_ref, ka_ref, rk_ref, bd_ref, tri_ref,
                    v_o, g_o, bonus_o,
                    at0_o, rt0_o, bb0_o, kb0_o, bh0_o, kh0_o, pl0_o,
                    at1_o, rt1_o, bb1_o, kb1_o, bh1_o, kh1_o, pl1_o):
    tm = z.shape[0]
    w = RWKV_WIDTH
    row = lax.broadcasted_iota(jnp.int32, (tm, 1), 0)
    zp = jnp.where(row == 0, prow, pltpu.roll(z, 1, 0))
    zn = jnp.where(row == tm - 1, nrow, pltpu.roll(z, tm - 1, 0))
    zs = z + (0.5 * (zp + zn) - z) * mu_ref[...]

    r = zs[:, 0:w]
    k = zs[:, w:2 * w]
    v = zs[:, 2 * w:3 * w]
    o3 = 3 * w
    wd = zs[:, o3:o3 + 2 * DECAY_LORA]
    ad = zs[:, o3 + 2 * DECAY_LORA:o3 + 2 * DECAY_LORA + 2 * ICLR_LORA]
    gd = zs[:, o3 + 2 * DECAY_LORA + 2 * ICLR_LORA:]
    bd = bd_ref[...]

    g_o[...] = _mm(_sigmoid(gd), g2_ref[...])
    v_o[...] = v.astype(BF16)
    kkf = k * kk_ref[...]
    kk = kkf * lax.rsqrt(jnp.maximum(_seg_sum(kkf * kkf, bd), 1e-12))
    wl_all = _mm(jnp.tanh(wd), w2_ref[...])
    al_all = _mm(ad, a2_ref[...])
    outs = ((at0_o, rt0_o, bb0_o, kb0_o, bh0_o, kh0_o, pl0_o),
            (at1_o, rt1_o, bb1_o, kb1_o, bh1_o, kh1_o, pl1_o))
    bonus = jnp.zeros((tm, w), F32)
    for d in range(2):
        at_o, rt_o, bb_o, kb_o, bh_o, kh_o, pl_o = outs[d]
        lw = -math.exp(-0.5) * _sigmoid(w0_ref[d:d + 1, :] + wl_all[:, d * w:(d + 1) * w])
        a = _sigmoid(a0_ref[d:d + 1, :] + al_all[:, d * w:(d + 1) * w])
        kd = k * (1.0 + (a - 1.0) * ka_ref[...])
        bonus = bonus + _seg_sum(r * kd * rk_ref[...], bd) * v
        cin = _chunk_cumsum(lw, tri_ref[d])
        cin3 = cin.reshape(tm // CHUNK, CHUNK, w)
        tot3 = cin3[:, CHUNK - 1:CHUNK, :] if d == 0 else cin3[:, 0:1, :]
        rem = (tot3 - cin3).reshape(tm, w)
        tot = tot3.reshape(tm // CHUNK, w)
        e_in = jnp.exp(cin)
        e_neg = jnp.exp(-cin)
        e_rem = jnp.exp(rem)
        beta = kk * a
        at_o[...] = (-kk * jnp.exp(cin - lw)).astype(BF16)
        rt_o[...] = (r * e_in).astype(BF16)
        bb_o[...] = (beta * e_neg).astype(BF16)
        kb_o[...] = (kd * e_neg).astype(BF16)
        bh_o[...] = (beta * e_rem).astype(BF16)
        kh_o[...] = (kd * e_rem).astype(BF16)
        pl_o[...] = jnp.exp(tot)
    bonus_o[...] = bonus


def _wkv_kernel(v0_ref, at0, rt0, bb0, kb0, bh0, kh0, pl0,
                v1_ref, at1, rt1, bb1, kb1, bh1, kh1, pl1,
                y0_ref, y1_ref, s_sc):
    c = pl.program_id(1)

    @pl.when(c == 0)
    def _():
        s_sc[...] = jnp.zeros(s_sc.shape, F32)

    n = 2 * CHUNK
    ri = lax.broadcasted_iota(jnp.int32, (n, n), 0)
    ci = lax.broadcasted_iota(jnp.int32, (n, n), 1)
    ti, tj = ri % CHUNK, ci % CHUNK
    same = (ri // CHUNK) == (ci // CHUNK)
    eye = ri == ci
    head0 = lax.broadcasted_iota(jnp.int32, (CHUNK, LANES), 1) < RWKV_HEAD_DIM

    def stack(x):
        zero = jnp.zeros_like(x)
        return jnp.concatenate([jnp.where(head0, x, zero), jnp.where(head0, zero, x)], axis=0)

    dirs = ((v0_ref, at0, rt0, bb0, kb0, bh0, kh0, pl0, y0_ref),
            (v1_ref, at1, rt1, bb1, kb1, bh1, kh1, pl1, y1_ref))
    units = []
    for d, (v_ref, at_r, rt_r, bb_r, kb_r, bh_r, kh_r, pl_r, y_ref) in enumerate(dirs):
        before = (ti > tj) if d == 0 else (ti < tj)
        strict = same & before
        incl = strict | eye
        levels = []
        b = 1
        while b < CHUNK:
            blk = same & ((ti // (2 * b)) == (tj // (2 * b)))
            hi, hj = (ti // b) % 2, (tj // b) % 2
            levels.append(blk & ((hi == 1) & (hj == 0) if d == 0 else (hi == 0) & (hj == 1)))
            b *= 2
        for p in range(RWKV_PAIRS):
            for j in (range(WKV_SUB) if d == 0 else range(WKV_SUB - 1, -1, -1)):
                units.append((d, p, slice(p * LANES, (p + 1) * LANES), strict, incl, levels,
                              (v_ref, at_r, rt_r, bb_r, kb_r, bh_r, kh_r, pl_r, y_ref),
                              j, slice(j * CHUNK, (j + 1) * CHUNK)))
    nu = range(len(units))

    def stacked(k):
        return [stack(u[6][k][u[8], u[2]]) for u in units]

    v_s, at_s, rt_s, bb_s, kb_s, bh_s, kh_s = (stacked(k) for k in range(7))
    gram = [_mm_nt(jnp.concatenate([at_s[i], rt_s[i]], axis=0), jnp.concatenate([bb_s[i], kb_s[i]], axis=0))
            for i in nu]
    a_ab = [jnp.where(units[i][3], gram[i][:n, :n], 0.0) for i in nu]
    a_ak = [jnp.where(units[i][3], gram[i][:n, n:], 0.0).astype(BF16) for i in nu]
    a_rb = [jnp.where(units[i][4], gram[i][n:, :n], 0.0).astype(BF16) for i in nu]
    a_rk = [jnp.where(units[i][4], gram[i][n:, n:], 0.0).astype(BF16) for i in nu]
    t_inv = [jnp.where(eye, 1.0, jnp.where(units[i][5][0], a_ab[i], 0.0)) for i in nu]
    for lv in range(1, len(units[0][5])):
        t_b = [t.astype(BF16) for t in t_inv]
        half = [_mm(t_b[i], jnp.where(units[i][5][lv], a_ab[i], 0.0)) for i in nu]
        t_inv = [t_inv[i] + _mm(half[i], t_b[i]) for i in nu]
    t_b = [t.astype(BF16) for t in t_inv]
    av = [_mm(jnp.concatenate([a_ak[i], a_rk[i]], axis=0), v_s[i]) for i in nu]
    akv = [av[i][:n].astype(BF16) for i in nu]
    wu = [_mm(t_b[i], jnp.concatenate([at_s[i], akv[i]], axis=1)).astype(BF16) for i in nu]
    qy = [_mm(a_rb[i], wu[i]) for i in nu]
    y0 = [qy[i][:, LANES:] + av[i][n:] for i in nu]
    qt = [rt_s[i].astype(F32) + qy[i][:, :LANES] for i in nu]
    mn = [_mm_tn(wu[i], bh_s[i]) for i in nu]
    n0t = [mn[i][LANES:, :] + _mm_tn(v_s[i], kh_s[i]) for i in nu]
    state = {(d, p): s_sc[d, p] for d in range(2) for p in range(RWKV_PAIRS)}
    for k in range(WKV_SUB):
        for i in nu:
            if i % WKV_SUB != k:
                continue
            d, p, sl, rows = units[i][0], units[i][1], units[i][2], units[i][8]
            pl_r, y_ref = units[i][6][7], units[i][6][8]
            st = state[d, p]
            ys = _mm_nt(qt[i], st) + y0[i]
            y_ref[rows, sl] = ys[:CHUNK] + ys[CHUNK:]
            state[d, p] = st * pl_r[units[i][7], :, sl] + _mm(st, mn[i][:LANES, :]) + n0t[i]
    for (d, p), st in state.items():
        s_sc[d, p] = st


def _wkv(v, dir0, dir1, batch, seq):
    m, w = v.shape
    nc = seq // (CHUNK * WKV_SUB)
    fmap = lambda b, c: (b * nc + c, 0)
    bmap = lambda b, c: (b * nc + nc - 1 - c, 0)
    fmap3 = lambda b, c: (b * nc + c, 0, 0)
    bmap3 = lambda b, c: (b * nc + nc - 1 - c, 0, 0)
    blk = lambda im: pl.BlockSpec((CHUNK * WKV_SUB, w), im)
    in_specs = ([blk(fmap)] * 7 + [pl.BlockSpec((WKV_SUB, 1, w), fmap3)]
                + [blk(bmap)] * 7 + [pl.BlockSpec((WKV_SUB, 1, w), bmap3)])
    return pl.pallas_call(
        _wkv_kernel,
        grid=(batch, nc),
        in_specs=in_specs,
        out_specs=[blk(fmap), blk(bmap)],
        out_shape=[jax.ShapeDtypeStruct((m, w), F32)] * 2,
        scratch_shapes=[pltpu.VMEM((2, RWKV_PAIRS, LANES, LANES), F32)],
        compiler_params=_params(("parallel", "arbitrary")),
        name="wkv_scan",
    )(v, *dir0, v, *dir1)


def _outproj_kernel(x_ref, ya_ref, yf_ref, yb_ref, bonus_ref, g_ref, gnw_ref, gnb_ref, bd_ref,
                    wout_ref, pg_ref, o_ref):
    bd = bd_ref[...]
    inv_n = 1.0 / RWKV_HEAD_DIM
    y = yf_ref[...] + yb_ref[...]
    mean = _seg_sum(y, bd) * inv_n
    yc = y - mean
    var = _seg_sum(yc * yc, bd) * inv_n
    yn = yc * lax.rsqrt(var + GN_EPS) * gnw_ref[...] + gnb_ref[...]
    yb = ((yn + bonus_ref[...]) * g_ref[...]).astype(BF16)
    mix = jnp.dot(jnp.concatenate([ya_ref[...], yb], axis=1), wout_ref[...], preferred_element_type=F32)
    o_ref[...] = x_ref[...] + _rms(mix, pg_ref[...])


def _out_proj(x2, ya, yf, yb, bonus, g, gn_w, gn_b, bd, wout, post_g):
    m, d = x2.shape
    tm = ROW_TILE
    w = RWKV_WIDTH
    row = lambda i: (i, 0)
    const = lambda i: (0, 0)
    full = lambda a: pl.BlockSpec(a.shape, const, pipeline_mode=pl.Buffered(1))
    return pl.pallas_call(
        _outproj_kernel,
        grid=(m // tm,),
        in_specs=[pl.BlockSpec((tm, d), row), pl.BlockSpec((tm, DIFF_WIDTH), row)]
                 + [pl.BlockSpec((tm, w), row)] * 4
                 + [full(gn_w), full(gn_b), full(bd), full(wout), full(post_g)],
        out_specs=pl.BlockSpec((tm, d), row),
        out_shape=jax.ShapeDtypeStruct((m, d), F32),
        compiler_params=_params(("parallel",)),
        name="out_proj",
    )(x2, ya, yf, yb, bonus, g, gn_w, gn_b, bd, wout, post_g)


def _gelu_tanh(x):
    c = math.sqrt(2.0 / math.pi)
    return 0.5 * x * (1.0 + jnp.tanh(c * (x + 0.044715 * (x * x * x))))


def _ffn_kernel(x_ref, xp_ref, xn_ref, g_ref, wup_ref, cw_ref, cb_ref, wdn_ref, pg_ref, o_ref, act_sc,
                *, tiles_per_seq, d_ff):
    tm = x_ref.shape[0]
    i = pl.program_id(0)
    first = (i % tiles_per_seq) == 0
    last = (i % tiles_per_seq) == tiles_per_seq - 1
    g = g_ref[...]
    x = x_ref[...]
    hb = _rms(x, g).astype(BF16)
    hp = _rms(xp_ref[...], g).astype(BF16)
    hn = _rms(xn_ref[...], g).astype(BF16)
    hext = jnp.concatenate([hp, hb, hn], axis=0)
    ext = tm + 2 * SUBLANES
    row = lax.broadcasted_iota(jnp.int32, (ext, 1), 0)
    kill_prev = jnp.logical_and(first, row == SUBLANES)
    kill_next = jnp.logical_and(last, row == SUBLANES + tm - 1)
    for j in range(d_ff // FF_CHUNK):
        cs = slice(j * FF_CHUNK, (j + 1) * FF_CHUNK)
        gate = jnp.dot(hext, wup_ref[:, cs], preferred_element_type=F32)
        up = jnp.dot(hb, wup_ref[:, d_ff + j * FF_CHUNK:d_ff + (j + 1) * FF_CHUNK],
                     preferred_element_type=F32)
        gp = jnp.where(kill_prev, 0.0, pltpu.roll(gate, 1, 0))
        gn = jnp.where(kill_next, 0.0, pltpu.roll(gate, ext - 1, 0))
        conv = cw_ref[0:1, cs] * gp + cw_ref[1:2, cs] * gate + cw_ref[2:3, cs] * gn + cb_ref[:, cs]
        conv = conv[SUBLANES:SUBLANES + tm, :]
        act_sc[:, cs] = (_gelu_tanh(conv) * up).astype(BF16)
    down = jnp.dot(act_sc[...], wdn_ref[...], preferred_element_type=F32)
    o_ref[...] = x + _rms(down, pg_ref[...])


def _ffn(x1, pre_g, wup, conv_w, conv_b, wdn, post_g, seq):
    m, d = x1.shape
    tm = min(FFN_ROW_TILE, seq)
    d_ff = wdn.shape[0]
    nblk8 = m // SUBLANES
    row = lambda i: (i, 0)
    const = lambda i: (0, 0)
    full = lambda a: pl.BlockSpec(a.shape, const, pipeline_mode=pl.Buffered(1))
    return pl.pallas_call(
        functools.partial(_ffn_kernel, tiles_per_seq=seq // tm, d_ff=d_ff),
        grid=(m // tm,),
        in_specs=[
            pl.BlockSpec((tm, d), row),
            pl.BlockSpec((SUBLANES, d), lambda i: (jnp.maximum(i * (tm // SUBLANES) - 1, 0), 0)),
            pl.BlockSpec((SUBLANES, d), lambda i: (jnp.minimum((i + 1) * (tm // SUBLANES), nblk8 - 1), 0)),
            full(pre_g), full(wup), full(conv_w), full(conv_b), full(wdn), full(post_g),
        ],
        out_specs=pl.BlockSpec((tm, d), row),
        out_shape=jax.ShapeDtypeStruct((m, d), F32),
        scratch_shapes=[pltpu.VMEM((tm, d_ff), BF16)],
        compiler_params=_params(("parallel",)),
        name="conv_ffn",
    )(x1, x1, x1, pre_g, wup, conv_w, conv_b, wdn, post_g)


def _rope_tables(seq):
    half = DIFF_QK_DIM // 2
    pos = jnp.arange(seq, dtype=F32)
    inv = ROPE_THETA ** (-jnp.arange(half, dtype=F32) / half)
    ang = pos[:, None] * inv[None, :]
    cos, sin = jnp.cos(ang), jnp.sin(ang)
    reps = LANES // DIFF_QK_DIM
    cos_t = jnp.tile(jnp.concatenate([cos, cos], axis=-1), (1, reps))
    sin_t = jnp.tile(jnp.concatenate([-sin, sin], axis=-1), (1, reps))
    return cos_t, sin_t


def _scan_constants():
    t = jnp.arange(CHUNK)
    tri = jnp.stack([t[:, None] >= t[None, :],
                     t[:, None] <= t[None, :]])
    ch = jnp.arange(LANES)
    bd = (ch[:, None] // RWKV_HEAD_DIM) == (ch[None, :] // RWKV_HEAD_DIM)
    return tri.astype(BF16), bd.astype(BF16)


def _lora_blockdiag(w2):
    z = jnp.zeros_like(w2[0])
    return jnp.concatenate([jnp.concatenate([w2[0], z], axis=1),
                            jnp.concatenate([z, w2[1]], axis=1)], axis=0).astype(BF16)


def kernel(x, pre_mix_norm, post_mix_norm, pre_ffn_norm, post_ffn_norm, w_in, diff_lambda_q1, diff_lambda_k1, diff_lambda_q2, diff_lambda_k2, diff_subln, rwkv_mu, rwkv_w0, rwkv_w2, rwkv_a0, rwkv_a2, rwkv_g2, rwkv_k_k, rwkv_k_a, rwkv_r_k, rwkv_gn_w, rwkv_gn_b, w_out, w_up, ffn_conv_w, ffn_conv_b, w_down):
    batch, seq, d = x.shape
    depth = w_in.shape[0]
    assert seq % ROW_TILE == 0 and seq % ATTN_TK == 0 and ROW_TILE % CHUNK == 0
    m = batch * seq
    nc_total = m // CHUNK
    cos_t, sin_t = _rope_tables(seq)
    tri, bd = _scan_constants()
    x2 = x.reshape(m, d)
    diff_cols = 3 * DIFF_WIDTH
    row2 = lambda a: a.reshape(1, -1)
    for l in range(depth):
        lambda_init = 0.8 - 0.6 * math.exp(-0.3 * l)
        wqk = w_in[l][:, :2 * DIFF_WIDTH].astype(BF16)
        wvt = w_in[l][:, 2 * DIFF_WIDTH:diff_cols].T.astype(BF16)
        wrw = w_in[l][:, diff_cols:].astype(BF16)
        prep_params = (row2(rwkv_mu[l]), rwkv_w0[l], _lora_blockdiag(rwkv_w2[l]), rwkv_a0[l],
                       _lora_blockdiag(rwkv_a2[l]), rwkv_g2[l].astype(BF16), row2(rwkv_k_k[l]),
                       row2(rwkv_k_a[l]), row2(rwkv_r_k[l]), bd, tri)
        q0, q1, k, vt, *prep = _in_proj(x2, row2(pre_mix_norm[l]), wqk, wvt, wrw, cos_t, sin_t, prep_params, seq)
        lam_p = jnp.stack([diff_lambda_q1[l], diff_lambda_k1[l], diff_lambda_q2[l], diff_lambda_k2[l]])
        ya = _attention(lam_p, diff_subln[l].reshape(-1, 1), q0, q1, k, vt, batch, seq, lambda_init)
        vb, g, bonus = prep[0], prep[1], prep[2]
        dir0 = list(prep[3:9]) + [prep[9].reshape(nc_total, 1, RWKV_WIDTH)]
        dir1 = list(prep[10:16]) + [prep[16].reshape(nc_total, 1, RWKV_WIDTH)]
        yf, yb = _wkv(vb, dir0, dir1, batch, seq)
        x2 = _out_proj(x2, ya, yf, yb, bonus, g, row2(rwkv_gn_w[l]), row2(rwkv_gn_b[l]), bd,
                       w_out[l].astype(BF16), row2(post_mix_norm[l]))
        x2 = _ffn(x2, row2(pre_ffn_norm[l]), w_up[l].astype(BF16), ffn_conv_w[l], row2(ffn_conv_b[l]),
                  w_down[l].astype(BF16), row2(post_ffn_norm[l]), seq)
    return x2.reshape(batch, seq, d)
```

```python
import functools
import math

import jax
import jax.numpy as jnp
from jax import lax
from jax.experimental import pallas as pl
from jax.experimental.pallas import tpu as pltpu

F32 = jnp.float32
BF16 = jnp.bfloat16

LANES = 128
SUBLANES = 8
VMEM_LIMIT = 56 * 1024 * 1024

DIFF_QK_DIM = 64
DIFF_V_DIM = 128
DIFF_HEADS = 4
DIFF_WIDTH = DIFF_HEADS * DIFF_V_DIM
RWKV_HEAD_DIM = 64
RWKV_WIDTH = 512
RWKV_PAIRS = RWKV_WIDTH // LANES
DECAY_LORA = 64
ICLR_LORA = 64
GATE_LORA = 128
ROPE_THETA = 10000.0
NORM_EPS = 1e-6
GN_EPS = 64e-5
CHUNK = 64
WKV_SUB = 2

ROW_TILE = 512
FFN_ROW_TILE = 1024
ATTN_TQ = 256
ATTN_TK = 512
ATTN_VROWS = 144
FF_CHUNK = 256


def _rms(x, g):
    return x * lax.rsqrt(jnp.mean(x * x, axis=-1, keepdims=True) + NORM_EPS) * g


def _mm(a, b):
    return jnp.dot(a.astype(BF16), b.astype(BF16), preferred_element_type=F32)


def _mm_nt(a, b):
    return lax.dot_general(a.astype(BF16), b.astype(BF16), (((1,), (1,)), ((), ())),
                           preferred_element_type=F32)


def _mm_tn(a, b):
    return lax.dot_general(a.astype(BF16), b.astype(BF16), (((0,), (0,)), ((), ())),
                           preferred_element_type=F32)


def _split2(x):
    hi = x.astype(BF16)
    lo = (x - hi.astype(F32)).astype(BF16)
    return hi, lo


def _seg_sum(x, bd):
    hi, lo = _split2(x)
    tiles = []
    for s in range(x.shape[1] // LANES):
        sl = slice(s * LANES, (s + 1) * LANES)
        tiles.append(jnp.dot(hi[:, sl], bd, preferred_element_type=F32)
                     + jnp.dot(lo[:, sl], bd, preferred_element_type=F32))
    return jnp.concatenate(tiles, axis=1)


def _chunk_cumsum(x, tri):
    hi, lo = _split2(x)
    chunks = []
    for c in range(x.shape[0] // CHUNK):
        rows = slice(c * CHUNK, (c + 1) * CHUNK)
        chunks.append(jnp.dot(tri, hi[rows], preferred_element_type=F32)
                      + jnp.dot(tri, lo[rows], preferred_element_type=F32))
    return jnp.concatenate(chunks, axis=0)


def _params(semantics):
    return pltpu.CompilerParams(dimension_semantics=semantics, vmem_limit_bytes=VMEM_LIMIT)


def _inproj_kernel(x_ref, xp_ref, xn_ref, g_ref, wqk_ref, wvt_ref, wrw_ref, cos_ref, sin_ref, *rest,
                   tiles_per_seq):
    prep_in, (q0_ref, q1_ref, k_ref, vt_ref), prep_out = rest[:11], rest[11:15], rest[15:]
    tm = x_ref.shape[0]
    hb = _rms(x_ref[...], g_ref[...]).astype(BF16)
    qk = jnp.dot(hb, wqk_ref[...], preferred_element_type=F32)
    cos = cos_ref[...]
    sin = sin_ref[...]
    lane = lax.broadcasted_iota(jnp.int32, (tm, LANES), 1)
    first_half = (lane % DIFF_QK_DIM) < (DIFF_QK_DIM // 2)
    comp0 = lane < DIFF_QK_DIM
    scale = DIFF_QK_DIM ** -0.5
    for j in range(2 * DIFF_HEADS):
        t = qk[:, j * LANES:(j + 1) * LANES]
        partner = jnp.where(first_half, pltpu.roll(t, LANES - DIFF_QK_DIM // 2, 1),
                            pltpu.roll(t, DIFF_QK_DIM // 2, 1))
        o = t * cos + partner * sin
        if j < DIFF_HEADS:
            o = o * scale
            q0_ref[:, j * LANES:(j + 1) * LANES] = jnp.where(comp0, o, 0.0).astype(BF16)
            q1_ref[:, j * LANES:(j + 1) * LANES] = jnp.where(comp0, 0.0, o).astype(BF16)
        else:
            jj = j - DIFF_HEADS
            k_ref[:, jj * LANES:(jj + 1) * LANES] = o.astype(BF16)
    vt = lax.dot_general(wvt_ref[...], hb, (((1,), (1,)), ((), ())), preferred_element_type=F32).astype(BF16)
    pad_rows = ATTN_VROWS - DIFF_V_DIM
    ones_row = lax.broadcasted_iota(jnp.int32, (pad_rows, tm), 0) == 0
    pad = jnp.where(ones_row, 1.0, 0.0).astype(BF16)
    for h in range(DIFF_HEADS):
        vt_ref[h * ATTN_VROWS:h * ATTN_VROWS + DIFF_V_DIM, :] = vt[h * DIFF_V_DIM:(h + 1) * DIFF_V_DIM]
        vt_ref[h * ATTN_VROWS + DIFF_V_DIM:(h + 1) * ATTN_VROWS, :] = pad
    z = jnp.dot(hb, wrw_ref[...], preferred_element_type=F32)
    halo = jnp.concatenate([xp_ref[...], xn_ref[...]], axis=0)
    z_halo = jnp.dot(_rms(halo, g_ref[...]).astype(BF16), wrw_ref[...], preferred_element_type=F32)
    i = pl.program_id(0)
    first = (i % tiles_per_seq) == 0
    last = (i % tiles_per_seq) == tiles_per_seq - 1
    prow = jnp.where(first, 0.0, z_halo[SUBLANES - 1:SUBLANES, :])
    nrow = jnp.where(last, 0.0, z_halo[SUBLANES:SUBLANES + 1, :])
    _rwkv_prep_body(z, prow, nrow, *prep_in, *prep_out)


def _in_proj(x2, g, wqk, wvt, wrw, cos_t, sin_t, prep_params, seq):
    m, d = x2.shape
    tm = ROW_TILE
    tps = seq // tm
    w = RWKV_WIDTH
    nblk8 = m // SUBLANES
    row = lambda i: (i, 0)
    const2 = lambda i: (0, 0)
    const3 = lambda i: (0, 0, 0)
    full = lambda a: pl.BlockSpec(a.shape, const2 if a.ndim == 2 else const3, pipeline_mode=pl.Buffered(1))
    big_bf = jax.ShapeDtypeStruct((m, w), BF16)
    big_f = jax.ShapeDtypeStruct((m, w), F32)
    per_dir_shapes = [big_bf] * 6 + [jax.ShapeDtypeStruct((m // CHUNK, w), F32)]
    per_dir_specs = [pl.BlockSpec((tm, w), row)] * 6 + [pl.BlockSpec((tm // CHUNK, w), row)]
    return pl.pallas_call(
        functools.partial(_inproj_kernel, tiles_per_seq=tps),
        grid=(m // tm,),
        in_specs=[
            pl.BlockSpec((tm, d), row),
            pl.BlockSpec((SUBLANES, d), lambda i: (jnp.maximum(i * (tm // SUBLANES) - 1, 0), 0)),
            pl.BlockSpec((SUBLANES, d), lambda i: (jnp.minimum((i + 1) * (tm // SUBLANES), nblk8 - 1), 0)),
            full(g), full(wqk), full(wvt), full(wrw),
            pl.BlockSpec((tm, LANES), lambda i: (i % tps, 0)),
            pl.BlockSpec((tm, LANES), lambda i: (i % tps, 0)),
        ] + [full(a) for a in prep_params],
        out_specs=[
            pl.BlockSpec((tm, DIFF_WIDTH), row),
            pl.BlockSpec((tm, DIFF_WIDTH), row),
            pl.BlockSpec((tm, DIFF_WIDTH), row),
            pl.BlockSpec((DIFF_HEADS * ATTN_VROWS, tm), lambda i: (i // tps, i % tps)),
        ] + [pl.BlockSpec((tm, w), row)] * 3 + per_dir_specs * 2,
        out_shape=[
            jax.ShapeDtypeStruct((m, DIFF_WIDTH), BF16),
            jax.ShapeDtypeStruct((m, DIFF_WIDTH), BF16),
            jax.ShapeDtypeStruct((m, DIFF_WIDTH), BF16),
            jax.ShapeDtypeStruct((m // seq * DIFF_HEADS * ATTN_VROWS, seq), BF16),
        ] + [big_bf, big_f, big_f] + per_dir_shapes * 2,
        compiler_params=_params(("parallel",)),
        name="in_proj",
    )(x2, x2, x2, g, wqk, wvt, wrw, cos_t, sin_t, *prep_params)


def _attn_kernel(lam_ref, sub_ref, q0_ref, q1_ref, q0n_ref, q1n_ref, k_ref, vt_ref, o_ref,
                 s_sc, mx_sc, m_sc, acc_sc, *, tk, lambda_init):
    seq = k_ref.shape[0]
    nk = seq // tk
    qs = (q0_ref[...], q1_ref[...])

    def scores(qpair, i, slot):
        start = i * tk
        ks = k_ref[pl.ds(start, tk), :]
        maxima = []
        for c in range(2):
            st = lax.dot_general(ks, qpair[c], (((1,), (1,)), ((), ())), preferred_element_type=F32)
            s_sc[slot, c] = st
            maxima.append(jnp.max(st, axis=0, keepdims=True))
        return tuple(maxima)

    @pl.when(pl.program_id(2) == 0)
    def _():
        first = scores(qs, 0, 0)
        mx_sc[0] = first[0]
        mx_sc[1] = first[1]

    m_sc[...] = jnp.full(m_sc.shape, -jnp.inf, F32)
    acc_sc[...] = jnp.zeros(acc_sc.shape, F32)

    def accumulate(i, slot, maxima):
        start = i * tk
        vt = vt_ref[:, pl.ds(start, tk)]
        for c in range(2):
            m_old = m_sc[c]
            m_new = jnp.maximum(m_old, maxima[c])
            alpha = jnp.exp(m_old - m_new)
            p = jnp.exp(s_sc[slot, c] - m_new).astype(BF16)
            acc_sc[c] = alpha * acc_sc[c] + jnp.dot(vt, p, preferred_element_type=F32)
            m_sc[c] = m_new

    maxima = (mx_sc[0], mx_sc[1])
    for blk in range(nk):
        if blk + 1 < nk:
            ahead = scores(qs, blk + 1, (blk + 1) % 2)
        else:
            ahead = scores((q0n_ref[...], q1n_ref[...]), 0, 0)
        accumulate(blk, blk % 2, maxima)
        maxima = ahead
    mx_sc[0] = maxima[0]
    mx_sc[1] = maxima[1]

    lp = lam_ref[...]
    lam = (jnp.exp(jnp.sum(lp[0:1] * lp[1:2], axis=-1, keepdims=True))
           - jnp.exp(jnp.sum(lp[2:3] * lp[3:4], axis=-1, keepdims=True)) + lambda_init)
    a0 = acc_sc[0]
    a1 = acc_sc[1]
    dv = DIFF_V_DIM
    o = a0[:dv] / a0[dv:dv + 1] - lam * (a1[:dv] / a1[dv:dv + 1])
    o = o * lax.rsqrt(jnp.mean(o * o, axis=0, keepdims=True) + NORM_EPS) * sub_ref[...] * (1.0 - lambda_init)
    o_ref[...] = o.T.astype(BF16)


def _attention(lam_p, subln_col, q0, q1, k, vt, batch, seq, lambda_init):
    tq, tk = ATTN_TQ, min(ATTN_TK, seq // 2)
    assert (seq // tk) % 2 == 0
    nq = seq // tq
    qmap = lambda b, h, i: (b * nq + i, h)
    qnext = lambda b, h, i: (b * nq + jnp.minimum(i + 1, nq - 1), h)
    const = lambda b, h, i: (0, 0)
    return pl.pallas_call(
        functools.partial(_attn_kernel, tk=tk, lambda_init=lambda_init),
        grid=(batch, DIFF_HEADS, nq),
        in_specs=[
            pl.BlockSpec(lam_p.shape, const),
            pl.BlockSpec(subln_col.shape, const),
            pl.BlockSpec((tq, LANES), qmap),
            pl.BlockSpec((tq, LANES), qmap),
            pl.BlockSpec((tq, LANES), qnext),
            pl.BlockSpec((tq, LANES), qnext),
            pl.BlockSpec((seq, LANES), lambda b, h, i: (b, h)),
            pl.BlockSpec((ATTN_VROWS, seq), lambda b, h, i: (b * DIFF_HEADS + h, 0)),
        ],
        out_specs=pl.BlockSpec((tq, LANES), qmap),
        out_shape=jax.ShapeDtypeStruct((batch * seq, DIFF_WIDTH), BF16),
        scratch_shapes=[
            pltpu.VMEM((2, 2, tk, tq), F32),
            pltpu.VMEM((2, 1, tq), F32),
            pltpu.VMEM((2, 1, tq), F32),
            pltpu.VMEM((2, ATTN_VROWS, tq), F32),
        ],
        compiler_params=_params(("parallel", "parallel", "arbitrary")),
        name="diff_attn",
    )(lam_p, subln_col, q0, q1, q0, q1, k, vt)


def _sigmoid(x):
    return 0.5 * jnp.tanh(0.5 * x) + 0.5


def _rwkv_prep_body(z, prow, nrow, mu_ref, w0_ref, w2_ref, a0_ref, a2_ref, g2_ref,
                    kk_ref, ka_ref, rk_ref, bd_ref, tri_ref,
                    v_o, g_o, bonus_o,
                    at0_o, rt0_o, bb0_o, kb0_o, bh0_o, kh0_o, pl0_o,
                    at1_o, rt1_o, bb1_o, kb1_o, bh1_o, kh1_o, pl1_o):
    tm = z.shape[0]
    w = RWKV_WIDTH
    row = lax.broadcasted_iota(jnp.int32, (tm, 1), 0)
    zp = jnp.where(row == 0, prow, pltpu.roll(z, 1, 0))
    zn = jnp.where(row == tm - 1, nrow, pltpu.roll(z, tm - 1, 0))
    zs = z + (0.5 * (zp + zn) - z) * mu_ref[...]

    r = zs[:, 0:w]
    k = zs[:, w:2 * w]
    v = zs[:, 2 * w:3 * w]
    o3 = 3 * w
    wd = zs[:, o3:o3 + 2 * DECAY_LORA]
    ad = zs[:, o3 + 2 * DECAY_LORA:o3 + 2 * DECAY_LORA + 2 * ICLR_LORA]
    gd = zs[:, o3 + 2 * DECAY_LORA + 2 * ICLR_LORA:]
    bd = bd_ref[...]

    g_o[...] = _mm(_sigmoid(gd), g2_ref[...])
    v_o[...] = v.astype(BF16)
    kkf = k * kk_ref[...]
    kk = kkf * lax.rsqrt(jnp.maximum(_seg_sum(kkf * kkf, bd), 1e-12))
    wl_all = _mm(jnp.tanh(wd), w2_ref[...])
    al_all = _mm(ad, a2_ref[...])
    outs = ((at0_o, rt0_o, bb0_o, kb0_o, bh0_o, kh0_o, pl0_o),
            (at1_o, rt1_o, bb1_o, kb1_o, bh1_o, kh1_o, pl1_o))
    bonus = jnp.zeros((tm, w), F32)
    for d in range(2):
        at_o, rt_o, bb_o, kb_o, bh_o, kh_o, pl_o = outs[d]
        lw = -math.exp(-0.5) * _sigmoid(w0_ref[d:d + 1, :] + wl_all[:, d * w:(d + 1) * w])
        a = _sigmoid(a0_ref[d:d + 1, :] + al_all[:, d * w:(d + 1) * w])
        kd = k * (1.0 + (a - 1.0) * ka_ref[...])
        bonus = bonus + _seg_sum(r * kd * rk_ref[...], bd) * v
        cin = _chunk_cumsum(lw, tri_ref[d])
        cin3 = cin.reshape(tm // CHUNK, CHUNK, w)
        tot3 = cin3[:, CHUNK - 1:CHUNK, :] if d == 0 else cin3[:, 0:1, :]
        rem = (tot3 - cin3).reshape(tm, w)
        tot = tot3.reshape(tm // CHUNK, w)
        e_in = jnp.exp(cin)
        e_neg = jnp.exp(-cin)
        e_rem = jnp.exp(rem)
        beta = kk * a
        at_o[...] = (-kk * jnp.exp(cin - lw)).astype(BF16)
        rt_o[...] = (r * e_in).astype(BF16)
        bb_o[...] = (beta * e_neg).astype(BF16)
        kb_o[...] = (kd * e_neg).astype(BF16)
        bh_o[...] = (beta * e_rem).astype(BF16)
        kh_o[...] = (kd * e_rem).astype(BF16)
        pl_o[...] = jnp.exp(tot)
    bonus_o[...] = bonus


def _wkv_kernel(v0_ref, at0, rt0, bb0, kb0, bh0, kh0, pl0,
                v1_ref, at1, rt1, bb1, kb1, bh1, kh1, pl1,
                y0_ref, y1_ref, s_sc):
    c = pl.program_id(1)

    @pl.when(c == 0)
    def _():
        s_sc[...] = jnp.zeros(s_sc.shape, F32)

    n = 2 * CHUNK
    ri = lax.broadcasted_iota(jnp.int32, (n, n), 0)
    ci = lax.broadcasted_iota(jnp.int32, (n, n), 1)
    ti, tj = ri % CHUNK, ci % CHUNK
    same = (ri // CHUNK) == (ci // CHUNK)
    eye = ri == ci
    head0 = lax.broadcasted_iota(jnp.int32, (CHUNK, LANES), 1) < RWKV_HEAD_DIM

    def stack(x):
        zero = jnp.zeros_like(x)
        return jnp.concatenate([jnp.where(head0, x, zero), jnp.where(head0, zero, x)], axis=0)

    dirs = ((v0_ref, at0, rt0, bb0, kb0, bh0, kh0, pl0, y0_ref),
            (v1_ref, at1, rt1, bb1, kb1, bh1, kh1, pl1, y1_ref))
    units = []
    for d, (v_ref, at_r, rt_r, bb_r, kb_r, bh_r, kh_r, pl_r, y_ref) in enumerate(dirs):
        before = (ti > tj) if d == 0 else (ti < tj)
        strict = same & before
        incl = strict | eye
        levels = []
        b = 1
        while b < CHUNK:
            blk = same & ((ti // (2 * b)) == (tj // (2 * b)))
            hi, hj = (ti // b) % 2, (tj // b) % 2
            levels.append(blk & ((hi == 1) & (hj == 0) if d == 0 else (hi == 0) & (hj == 1)))
            b *= 2
        for p in range(RWKV_PAIRS):
            for j in (range(WKV_SUB) if d == 0 else range(WKV_SUB - 1, -1, -1)):
                units.append((d, p, slice(p * LANES, (p + 1) * LANES), strict, incl, levels,
                              (v_ref, at_r, rt_r, bb_r, kb_r, bh_r, kh_r, pl_r, y_ref),
                              j, slice(j * CHUNK, (j + 1) * CHUNK)))
    nu = range(len(units))

    def stacked(k):
        return [stack(u[6][k][u[8], u[2]]) for u in units]

    v_s, at_s, rt_s, bb_s, kb_s, bh_s, kh_s = (stacked(k) for k in range(7))
    gram = [_mm_nt(jnp.concatenate([at_s[i], rt_s[i]], axis=0), jnp.concatenate([bb_s[i], kb_s[i]], axis=0))
            for i in nu]
    a_ab = [jnp.where(units[i][3], gram[i][:n, :n], 0.0) for i in nu]
    a_ak = [jnp.where(units[i][3], gram[i][:n, n:], 0.0).astype(BF16) for i in nu]
    a_rb = [jnp.where(units[i][4], gram[i][n:, :n], 0.0).astype(BF16) for i in nu]
    a_rk = [jnp.where(units[i][4], gram[i][n:, n:], 0.0).astype(BF16) for i in nu]
    t_inv = [jnp.where(eye, 1.0, jnp.where(units[i][5][0], a_ab[i], 0.0)) for i in nu]
    for lv in range(1, len(units[0][5])):
        t_b = [t.astype(BF16) for t in t_inv]
        half = [_mm(t_b[i], jnp.where(units[i][5][lv], a_ab[i], 0.0)) for i in nu]
        t_inv = [t_inv[i] + _mm(half[i], t_b[i]) for i in nu]
    t_b = [t.astype(BF16) for t in t_inv]
    av = [_mm(jnp.concatenate([a_ak[i], a_rk[i]], axis=0), v_s[i]) for i in nu]
    akv = [av[i][:n].astype(BF16) for i in nu]
    wu = [_mm(t_b[i], jnp.concatenate([at_s[i], akv[i]], axis=1)).astype(BF16) for i in nu]
    qy = [_mm(a_rb[i], wu[i]) for i in nu]
    y0 = [qy[i][:, LANES:] + av[i][n:] for i in nu]
    qt = [rt_s[i].astype(F32) + qy[i][:, :LANES] for i in nu]
    mn = [_mm_tn(wu[i], bh_s[i]) for i in nu]
    n0t = [mn[i][LANES:, :] + _mm_tn(v_s[i], kh_s[i]) for i in nu]
    state = {(d, p): s_sc[d, p] for d in range(2) for p in range(RWKV_PAIRS)}
    for k in range(WKV_SUB):
        for i in nu:
            if i % WKV_SUB != k:
                continue
            d, p, sl, rows = units[i][0], units[i][1], units[i][2], units[i][8]
            pl_r, y_ref = units[i][6][7], units[i][6][8]
            st = state[d, p]
            ys = _mm_nt(qt[i], st) + y0[i]
            y_ref[rows, sl] = ys[:CHUNK] + ys[CHUNK:]
            state[d, p] = st * pl_r[units[i][7], :, sl] + _mm(st, mn[i][:LANES, :]) + n0t[i]
    for (d, p), st in state.items():
        s_sc[d, p] = st


def _wkv(v, dir0, dir1, batch, seq):
    m, w = v.shape
    nc = seq // (CHUNK * WKV_SUB)
    fmap = lambda b, c: (b * nc + c, 0)
    bmap = lambda b, c: (b * nc + nc - 1 - c, 0)
    fmap3 = lambda b, c: (b * nc + c, 0, 0)
    bmap3 = lambda b, c: (b * nc + nc - 1 - c, 0, 0)
    blk = lambda im: pl.BlockSpec((CHUNK * WKV_SUB, w), im)
    in_specs = ([blk(fmap)] * 7 + [pl.BlockSpec((WKV_SUB, 1, w), fmap3)]
                + [blk(bmap)] * 7 + [pl.BlockSpec((WKV_SUB, 1, w), bmap3)])
    return pl.pallas_call(
        _wkv_kernel,
        grid=(batch, nc),
        in_specs=in_specs,
        out_specs=[blk(fmap), blk(bmap)],
        out_shape=[jax.ShapeDtypeStruct((m, w), F32)] * 2,
        scratch_shapes=[pltpu.VMEM((2, RWKV_PAIRS, LANES, LANES), F32)],
        compiler_params=_params(("parallel", "arbitrary")),
        name="wkv_scan",
    )(v, *dir0, v, *dir1)


def _outproj_kernel(x_ref, ya_ref, yf_ref, yb_ref, bonus_ref, g_ref, gnw_ref, gnb_ref, bd_ref,
                    wout_ref, pg_ref, o_ref):
    bd = bd_ref[...]
    inv_n = 1.0 / RWKV_HEAD_DIM
    y = yf_ref[...] + yb_ref[...]
    mean = _seg_sum(y, bd) * inv_n
    yc = y - mean
    var = _seg_sum(yc * yc, bd) * inv_n
    yn = yc * lax.rsqrt(var + GN_EPS) * gnw_ref[...] + gnb_ref[...]
    yb = ((yn + bonus_ref[...]) * g_ref[...]).astype(BF16)
    mix = jnp.dot(jnp.concatenate([ya_ref[...], yb], axis=1), wout_ref[...], preferred_element_type=F32)
    o_ref[...] = x_ref[...] + _rms(mix, pg_ref[...])


def _out_proj(x2, ya, yf, yb, bonus, g, gn_w, gn_b, bd, wout, post_g):
    m, d = x2.shape
    tm = ROW_TILE
    w = RWKV_WIDTH
    row = lambda i: (i, 0)
    const = lambda i: (0, 0)
    full = lambda a: pl.BlockSpec(a.shape, const, pipeline_mode=pl.Buffered(1))
    return pl.pallas_call(
        _outproj_kernel,
        grid=(m // tm,),
        in_specs=[pl.BlockSpec((tm, d), row), pl.BlockSpec((tm, DIFF_WIDTH), row)]
                 + [pl.BlockSpec((tm, w), row)] * 4
                 + [full(gn_w), full(gn_b), full(bd), full(wout), full(post_g)],
        out_specs=pl.BlockSpec((tm, d), row),
        out_shape=jax.ShapeDtypeStruct((m, d), F32),
        compiler_params=_params(("parallel",)),
        name="out_proj",
    )(x2, ya, yf, yb, bonus, g, gn_w, gn_b, bd, wout, post_g)


def _gelu_tanh(x):
    c = math.sqrt(2.0 / math.pi)
    return 0.5 * x * (1.0 + jnp.tanh(c * (x + 0.044715 * (x * x * x))))


def _ffn_kernel(x_ref, xp_ref, xn_ref, g_ref, wup_ref, cw_ref, cb_ref, wdn_ref, pg_ref, o_ref, act_sc,
                *, tiles_per_seq, d_ff):
    tm = x_ref.shape[0]
    i = pl.program_id(0)
    first = (i % tiles_per_seq) == 0
    last = (i % tiles_per_seq) == tiles_per_seq - 1
    g = g_ref[...]
    x = x_ref[...]
    hb = _rms(x, g).astype(BF16)
    hp = _rms(xp_ref[...], g).astype(BF16)
    hn = _rms(xn_ref[...], g).astype(BF16)
    hext = jnp.concatenate([hp, hb, hn], axis=0)
    ext = tm + 2 * SUBLANES
    row = lax.broadcasted_iota(jnp.int32, (ext, 1), 0)
    kill_prev = jnp.logical_and(first, row == SUBLANES)
    kill_next = jnp.logical_and(last, row == SUBLANES + tm - 1)
    for j in range(d_ff // FF_CHUNK):
        cs = slice(j * FF_CHUNK, (j + 1) * FF_CHUNK)
        gate = jnp.dot(hext, wup_ref[:, cs], preferred_element_type=F32)
        up = jnp.dot(hb, wup_ref[:, d_ff + j * FF_CHUNK:d_ff + (j + 1) * FF_CHUNK],
                     preferred_element_type=F32)
        gp = jnp.where(kill_prev, 0.0, pltpu.roll(gate, 1, 0))
        gn = jnp.where(kill_next, 0.0, pltpu.roll(gate, ext - 1, 0))
        conv = cw_ref[0:1, cs] * gp + cw_ref[1:2, cs] * gate + cw_ref[2:3, cs] * gn + cb_ref[:, cs]
        conv = conv[SUBLANES:SUBLANES + tm, :]
        act_sc[:, cs] = (_gelu_tanh(conv) * up).astype(BF16)
    down = jnp.dot(act_sc[...], wdn_ref[...], preferred_element_type=F32)
    o_ref[...] = x + _rms(down, pg_ref[...])


def _ffn(x1, pre_g, wup, conv_w, conv_b, wdn, post_g, seq):
    m, d = x1.shape
    tm = min(FFN_ROW_TILE, seq)
    d_ff = wdn.shape[0]
    nblk8 = m // SUBLANES
    row = lambda i: (i, 0)
    const = lambda i: (0, 0)
    full = lambda a: pl.BlockSpec(a.shape, const, pipeline_mode=pl.Buffered(1))
    return pl.pallas_call(
        functools.partial(_ffn_kernel, tiles_per_seq=seq // tm, d_ff=d_ff),
        grid=(m // tm,),
        in_specs=[
            pl.BlockSpec((tm, d), row),
            pl.BlockSpec((SUBLANES, d), lambda i: (jnp.maximum(i * (tm // SUBLANES) - 1, 0), 0)),
            pl.BlockSpec((SUBLANES, d), lambda i: (jnp.minimum((i + 1) * (tm // SUBLANES), nblk8 - 1), 0)),
            full(pre_g), full(wup), full(conv_w), full(conv_b), full(wdn), full(post_g),
        ],
        out_specs=pl.BlockSpec((tm, d), row),
        out_shape=jax.ShapeDtypeStruct((m, d), F32),
        scratch_shapes=[pltpu.VMEM((tm, d_ff), BF16)],
        compiler_params=_params(("parallel",)),
        name="conv_ffn",
    )(x1, x1, x1, pre_g, wup, conv_w, conv_b, wdn, post_g)


def _rope_tables(seq):
    half = DIFF_QK_DIM // 2
    pos = jnp.arange(seq, dtype=F32)
    inv = ROPE_THETA ** (-jnp.arange(half, dtype=F32) / half)
    ang = pos[:, None] * inv[None, :]
    cos, sin = jnp.cos(ang), jnp.sin(ang)
    reps = LANES // DIFF_QK_DIM
    cos_t = jnp.tile(jnp.concatenate([cos, cos], axis=-1), (1, reps))
    sin_t = jnp.tile(jnp.concatenate([-sin, sin], axis=-1), (1, reps))
    return cos_t, sin_t


def _scan_constants():
    t = jnp.arange(CHUNK)
    tri = jnp.stack([t[:, None] >= t[None, :],
                     t[:, None] <= t[None, :]])
    ch = jnp.arange(LANES)
    bd = (ch[:, None] // RWKV_HEAD_DIM) == (ch[None, :] // RWKV_HEAD_DIM)
    return tri.astype(BF16), bd.astype(BF16)


def _lora_blockdiag(w2):
    z = jnp.zeros_like(w2[0])
    return jnp.concatenate([jnp.concatenate([w2[0], z], axis=1),
                            jnp.concatenate([z, w2[1]], axis=1)], axis=0).astype(BF16)


def kernel(x, pre_mix_norm, post_mix_norm, pre_ffn_norm, post_ffn_norm, w_in, diff_lambda_q1, diff_lambda_k1, diff_lambda_q2, diff_lambda_k2, diff_subln, rwkv_mu, rwkv_w0, rwkv_w2, rwkv_a0, rwkv_a2, rwkv_g2, rwkv_k_k, rwkv_k_a, rwkv_r_k, rwkv_gn_w, rwkv_gn_b, w_out, w_up, ffn_conv_w, ffn_conv_b, w_down):
    batch, seq, d = x.shape
    depth = w_in.shape[0]
    assert seq % ROW_TILE == 0 and seq % ATTN_TK == 0 and ROW_TILE % CHUNK == 0
    m = batch * seq
    nc_total = m // CHUNK
    cos_t, sin_t = _rope_tables(seq)
    tri, bd = _scan_constants()
    x2 = x.reshape(m, d)
    diff_cols = 3 * DIFF_WIDTH
    row2 = lambda a: a.reshape(1, -1)
    for l in range(depth):
        lambda_init = 0.8 - 0.6 * math.exp(-0.3 * l)
        wqk = w_in[l][:, :2 * DIFF_WIDTH].astype(BF16)
        wvt = w_in[l][:, 2 * DIFF_WIDTH:diff_cols].T.astype(BF16)
        wrw = w_in[l][:, diff_cols:].astype(BF16)
        prep_params = (row2(rwkv_mu[l]), rwkv_w0[l], _lora_blockdiag(rwkv_w2[l]), rwkv_a0[l],
                       _lora_blockdiag(rwkv_a2[l]), rwkv_g2[l].astype(BF16), row2(rwkv_k_k[l]),
                       row2(rwkv_k_a[l]), row2(rwkv_r_k[l]), bd, tri)
        q0, q1, k, vt, *prep = _in_proj(x2, row2(pre_mix_norm[l]), wqk, wvt, wrw, cos_t, sin_t, prep_params, seq)
        lam_p = jnp.stack([diff_lambda_q1[l], diff_lambda_k1[l], diff_lambda_q2[l], diff_lambda_k2[l]])
        ya = _attention(lam_p, diff_subln[l].reshape(-1, 1), q0, q1, k, vt, batch, seq, lambda_init)
        vb, g, bonus = prep[0], prep[1], prep[2]
        dir0 = list(prep[3:9]) + [prep[9].reshape(nc_total, 1, RWKV_WIDTH)]
        dir1 = list(prep[10:16]) + [prep[16].reshape(nc_total, 1, RWKV_WIDTH)]
        yf, yb = _wkv(vb, dir0, dir1, batch, seq)
        x2 = _out_proj(x2, ya, yf, yb, bonus, g, row2(rwkv_gn_w[l]), row2(rwkv_gn_b[l]), bd,
                       w_out[l].astype(BF16), row2(post_mix_norm[l]))
        x2 = _ffn(x2, row2(pre_ffn_norm[l]), w_up[l].astype(BF16), ffn_conv_w[l], row2(ffn_conv_b[l]),
                  w_down[l].astype(BF16), row2(post_ffn_norm[l]), seq)
    return x2.reshape(batch, seq, d)
```
